```python
import math
import jax
import jax.numpy as jnp
from jax import lax
import numpy as np

D_MODEL = 1024
BATCH = 32
SEQ = 2048
DEPTH = 4

N_A_LAYERS = max(1, DEPTH // 2)
N_B_LAYERS = DEPTH - N_A_LAYERS

SSM_GROUP = 16
SSM_GROUPS = D_MODEL // SSM_GROUP
SSM_STATE = 64
STEP_MIN = 1e-3
STEP_MAX = 1e-1

N_HEADS = 16
QK_NOPE_DIM = 64
QK_ROPE_DIM = 32
QK_DIM = QK_NOPE_DIM + QK_ROPE_DIM
V_DIM = 64
Q_LORA_RANK = 384
KV_LORA_RANK = 256
ROPE_THETA = 10000.0
Q_BLOCK = 128

D_FF = -(-8 * D_MODEL // (3 * 256)) * 256

EPS = 1e-6
NEG_INF = -1e30

kernel_name = 'yoco_s5_mla_hybrid'


def rms_norm(x, gain):
    xf = x.astype(jnp.float32)
    y = xf * lax.rsqrt(jnp.mean(xf * xf, axis=-1, keepdims=True) + EPS)
    return (y * gain.astype(jnp.float32)).astype(x.dtype)


def swiglu_ffn(h, w_gate_up, w_down):
    gate, up = jnp.split(h @ w_gate_up, 2, axis=-1)
    return (jax.nn.silu(gate) * up) @ w_down


def cmul(ar, ai, br, bi):
    return ar * br - ai * bi, ar * bi + ai * br


def s5_mixer(h, w_in, lam_re, lam_im, log_step, b_re, b_im, c_re, c_im, d_skip, w_glu):
    bsz, seq, _ = h.shape
    f32 = jnp.float32
    u = (h @ w_in).reshape(bsz, seq, SSM_GROUPS, SSM_GROUP).astype(f32)
    step = jnp.exp(log_step.astype(f32))[:, None]
    lr, li = lam_re.astype(f32), lam_im.astype(f32)
    decay = jnp.exp(lr * step)
    a_re, a_im = decay * jnp.cos(li * step), decay * jnp.sin(li * step)
    den = lr * lr + li * li
    f_re = ((a_re - 1.0) * lr + a_im * li) / den
    f_im = (a_im * lr - (a_re - 1.0) * li) / den
    br, bi = cmul(f_re[..., None], f_im[..., None], b_re.astype(f32), b_im.astype(f32))
    cr, ci = c_re.astype(f32), c_im.astype(f32)

    def combine(e_i, e_j):
        ar_i, ai_i, xr_i, xi_i = e_i
        ar_j, ai_j, xr_j, xi_j = e_j
        ar, ai = cmul(ar_j, ai_j, ar_i, ai_i)
        xr, xi = cmul(ar_j, ai_j, xr_i, xi_i)
        return ar, ai, xr + xr_j, xi + xi_j

    def scan_sequence(u_seq):
        bu_re = jnp.einsum('gnk,lgk->lgn', br, u_seq)
        bu_im = jnp.einsum('gnk,lgk->lgn', bi, u_seq)
        ar_seq = jnp.broadcast_to(a_re, bu_re.shape)
        ai_seq = jnp.broadcast_to(a_im, bu_im.shape)
        _, _, xr, xi = lax.associative_scan(combine, (ar_seq, ai_seq, bu_re, bu_im), axis=0)
        return jnp.einsum('gkn,lgn->lgk', cr, xr) - jnp.einsum('gkn,lgn->lgk', ci, xi)

    y = lax.map(scan_sequence, u)
    y = y + d_skip.astype(f32).reshape(SSM_GROUPS, SSM_GROUP) * u
    z = jax.nn.gelu(y.reshape(bsz, seq, D_MODEL)).astype(h.dtype)
    val, gate = jnp.split(z @ w_glu, 2, axis=-1)
    return val * jax.nn.sigmoid(gate)


def rope_tables(positions):
    inv_freq = ROPE_THETA ** (-jnp.arange(0, QK_ROPE_DIM, 2, dtype=jnp.float32) / QK_ROPE_DIM)
    ang = positions.astype(jnp.float32)[..., None] * inv_freq
    return jnp.cos(ang), jnp.sin(ang)


def apply_rope(x, cos, sin):
    x1, x2 = jnp.split(x.astype(jnp.float32), 2, axis=-1)
    return jnp.concatenate([x1 * cos - x2 * sin, x1 * sin + x2 * cos], axis=-1).astype(x.dtype)


def mla_shared_kv(h, w_kv_a, kv_a_norm, w_kv_b, k_nope_norm, k_rope_norm, cos, sin):
    bsz, seq, _ = h.shape
    ckv = h @ w_kv_a
    c_kv = rms_norm(ckv[..., :KV_LORA_RANK], kv_a_norm)
    k_rope = apply_rope(rms_norm(ckv[..., KV_LORA_RANK:], k_rope_norm), cos, sin)
    kv = (c_kv @ w_kv_b).reshape(bsz, seq, N_HEADS, QK_NOPE_DIM + V_DIM)
    k_nope = rms_norm(kv[..., :QK_NOPE_DIM], k_nope_norm)
    v = kv[..., QK_NOPE_DIM:]
    k_rope = jnp.broadcast_to(k_rope[:, :, None, :], (bsz, seq, N_HEADS, QK_ROPE_DIM))
    k = jnp.concatenate([k_nope, k_rope], axis=-1)
    return k.transpose(0, 2, 1, 3), v.transpose(0, 2, 1, 3)


def mla_queries(h, w_q_a, q_a_norm, w_q_b, q_nope_norm, q_rope_norm, cos, sin):
    bsz, seq, _ = h.shape
    c_q = rms_norm(h @ w_q_a, q_a_norm)
    q = (c_q @ w_q_b).reshape(bsz, seq, N_HEADS, QK_DIM)
    q_nope = rms_norm(q[..., :QK_NOPE_DIM], q_nope_norm)
    q_rope = apply_rope(rms_norm(q[..., QK_NOPE_DIM:], q_rope_norm),
                        cos[:, :, None, :], sin[:, :, None, :])
    return jnp.concatenate([q_nope, q_rope], axis=-1).transpose(0, 2, 1, 3)


def causal_block_attention(q, k, v):
    bsz, n_heads, seq, dk = q.shape
    dv = v.shape[-1]
    n_blocks = seq // Q_BLOCK
    q_blocks = q.reshape(bsz, n_heads, n_blocks, Q_BLOCK, dk).transpose(2, 0, 1, 3, 4)
    scale = 1.0 / math.sqrt(dk)
    key_pos = jnp.arange(seq)

    def one_block(args):
        q_blk, blk_idx = args
        s = jnp.einsum('bhqd,bhkd->bhqk', q_blk, k).astype(jnp.float32) * scale
        query_pos = blk_idx * Q_BLOCK + jnp.arange(Q_BLOCK)
        s = jnp.where(key_pos[None, :] <= query_pos[:, None], s, NEG_INF)
        p = jax.nn.softmax(s, axis=-1).astype(v.dtype)
        return jnp.einsum('bhqk,bhkd->bhqd', p, v)

    o = lax.map(one_block, (q_blocks, jnp.arange(n_blocks)))
    return o.transpose(1, 2, 0, 3, 4).reshape(bsz, n_heads, seq, dv)


def setup_inputs(seed: int = 0) -> dict:
    key = jax.random.key(seed)
    ks = jax.random.split(key, 32)
    f32 = jnp.float32

    def dense(k, shape, fan_in):
        return jax.random.normal(k, shape, f32) * fan_in ** -0.5

    def gain(k, shape):
        return 1.0 + 0.02 * jax.random.normal(k, shape, f32)

    na, nb = N_A_LAYERS, N_B_LAYERS
    g, n, kk = SSM_GROUPS, SSM_STATE, SSM_GROUP
    x = jax.random.normal(ks[0], (BATCH, SEQ, D_MODEL), f32)
    positions = (jnp.arange(SEQ, dtype=jnp.int32)[None, :]
                 + jax.random.randint(ks[1], (BATCH, 1), 0, SEQ, dtype=jnp.int32))
    mix_norm = gain(ks[2], (DEPTH, D_MODEL))
    ffn_norm = gain(ks[3], (DEPTH, D_MODEL))
    ffn_w_gate_up = dense(ks[4], (DEPTH, D_MODEL, 2 * D_FF), D_MODEL)
    ffn_w_down = dense(ks[5], (DEPTH, D_FF, D_MODEL), D_FF)
    ssm_w_in = dense(ks[6], (na, D_MODEL, D_MODEL), D_MODEL)
    ssm_lambda_re = -0.5 + 0.01 * jax.random.normal(ks[7], (na, g, n), f32)
    ssm_lambda_im = (jnp.pi * jnp.arange(n, dtype=f32)
                     + 0.01 * jax.random.normal(ks[8], (na, g, n), f32))
    ssm_log_step = jax.random.uniform(ks[9], (na, g), f32,
                                      minval=math.log(STEP_MIN), maxval=math.log(STEP_MAX))
    ssm_b_re = dense(ks[10], (na, g, n, kk), 2 * kk)
    ssm_b_im = dense(ks[11], (na, g, n, kk), 2 * kk)
    ssm_c_re = dense(ks[12], (na, g, kk, n), 2 * n)
    ssm_c_im = dense(ks[13], (na, g, kk, n), 2 * n)
    ssm_d = jax.random.normal(ks[14], (na, D_MODEL), f32)
    ssm_w_glu = dense(ks[15], (na, D_MODEL, 2 * D_MODEL), D_MODEL)
    kv_in_norm = gain(ks[16], (D_MODEL,))
    mla_w_kv_a = dense(ks[17], (D_MODEL, KV_LORA_RANK + QK_ROPE_DIM), D_MODEL)
    mla_kv_a_norm = gain(ks[18], (KV_LORA_RANK,))
    mla_w_kv_b = dense(ks[19], (KV_LORA_RANK, N_HEADS * (QK_NOPE_DIM + V_DIM)), KV_LORA_RANK)
    mla_k_nope_norm = gain(ks[20], (QK_NOPE_DIM,))
    mla_k_rope_norm = gain(ks[21], (QK_ROPE_DIM,))
    mla_w_q_a = dense(ks[22], (nb, D_MODEL, Q_LORA_RANK), D_MODEL)
    mla_q_a_norm = gain(ks[23], (nb, Q_LORA_RANK))
    mla_w_q_b = dense(ks[24], (nb, Q_LORA_RANK, N_HEADS * QK_DIM), Q_LORA_RANK)
    mla_q_nope_norm = gain(ks[25], (nb, QK_NOPE_DIM))
    mla_q_rope_norm = gain(ks[26], (nb, QK_ROPE_DIM))
    mla_w_o = dense(ks[27], (nb, N_HEADS * V_DIM, D_MODEL), N_HEADS * V_DIM)
    return {
        'x': x, 'positions': positions,
        'mix_norm': mix_norm, 'ffn_norm': ffn_norm,
        'ffn_w_gate_up': ffn_w_gate_up, 'ffn_w_down': ffn_w_down,
        'ssm_w_in': ssm_w_in, 'ssm_lambda_re': ssm_lambda_re, 'ssm_lambda_im': ssm_lambda_im,
        'ssm_log_step': ssm_log_step, 'ssm_b_re': ssm_b_re, 'ssm_b_im': ssm_b_im,
        'ssm_c_re': ssm_c_re, 'ssm_c_im': ssm_c_im, 'ssm_d': ssm_d, 'ssm_w_glu': ssm_w_glu,
        'kv_in_norm': kv_in_norm, 'mla_w_kv_a': mla_w_kv_a, 'mla_kv_a_norm': mla_kv_a_norm,
        'mla_w_kv_b': mla_w_kv_b, 'mla_k_nope_norm': mla_k_nope_norm,
        'mla_k_rope_norm': mla_k_rope_norm,
        'mla_w_q_a': mla_w_q_a, 'mla_q_a_norm': mla_q_a_norm, 'mla_w_q_b': mla_w_q_b,
        'mla_q_nope_norm': mla_q_nope_norm, 'mla_q_rope_norm': mla_q_rope_norm,
        'mla_w_o': mla_w_o,
    }


def reference(x, positions, mix_norm, ffn_norm, ffn_w_gate_up, ffn_w_down,
              ssm_w_in, ssm_lambda_re, ssm_lambda_im, ssm_log_step, ssm_b_re, ssm_b_im,
              ssm_c_re, ssm_c_im, ssm_d, ssm_w_glu,
              kv_in_norm, mla_w_kv_a, mla_kv_a_norm, mla_w_kv_b, mla_k_nope_norm,
              mla_k_rope_norm, mla_w_q_a, mla_q_a_norm, mla_w_q_b, mla_q_nope_norm,
              mla_q_rope_norm, mla_w_o):
    bsz, seq, _ = x.shape
    cos, sin = rope_tables(positions)
    k_shared, v_shared = None, None
    for layer in range(DEPTH):
        h = rms_norm(x, mix_norm[layer])
        if layer < N_A_LAYERS:
            i = layer
            x = x + s5_mixer(h, ssm_w_in[i], ssm_lambda_re[i], ssm_lambda_im[i], ssm_log_step[i],
                             ssm_b_re[i], ssm_b_im[i], ssm_c_re[i], ssm_c_im[i], ssm_d[i],
                             ssm_w_glu[i])
        else:
            j = layer - N_A_LAYERS
            if j == 0:
                k_shared, v_shared = mla_shared_kv(rms_norm(x, kv_in_norm), mla_w_kv_a,
                                                   mla_kv_a_norm, mla_w_kv_b, mla_k_nope_norm,
                                                   mla_k_rope_norm, cos, sin)
            q = mla_queries(h, mla_w_q_a[j], mla_q_a_norm[j], mla_w_q_b[j],
                            mla_q_nope_norm[j], mla_q_rope_norm[j], cos, sin)
            o = causal_block_attention(q, k_shared, v_shared)
            o = o.transpose(0, 2, 1, 3).reshape(bsz, seq, N_HEADS * V_DIM)
            x = x + o @ mla_w_o[j]
        x = x + swiglu_ffn(rms_norm(x, ffn_norm[layer]), ffn_w_gate_up[layer], ffn_w_down[layer])
    return x
```

```python
import functools
import math

import jax
import jax.numpy as jnp
from jax import lax
from jax.experimental import pallas as pl
from jax.experimental.pallas import tpu as pltpu

F32 = jnp.float32
BF16 = jnp.bfloat16

EPS = 1e-6
NEG_INF = -1e30
ROPE_THETA = 10000.0

SSM_GROUP = 16
SSM_STATE = 64
N_HEADS = 16
QK_NOPE = 64
QK_ROPE = 32
HALF_ROPE = QK_ROPE // 2
QK_DIM = QK_NOPE + QK_ROPE
V_DIM = 64
KV_LORA = 256

LANES = 128
MXU_TILE = 256
VMEM_LIMIT = 56 * 1024 * 1024

HEAD_TILE = LANES
CH_TILE = MXU_TILE
GROUPS_PER_TILE = CH_TILE // SSM_GROUP
STATES_PER_TILE = GROUPS_PER_TILE * SSM_STATE


def _rms(x, gain):
    ms = jnp.mean(x * x, axis=-1, keepdims=True)
    return x * lax.rsqrt(ms + EPS) * gain


def _const_spec(shape):
    nd = len(shape)
    return pl.BlockSpec(shape, lambda *_: (0,) * nd, pipeline_mode=pl.Buffered(1))


def _params(semantics):
    return pltpu.CompilerParams(dimension_semantics=semantics, vmem_limit_bytes=VMEM_LIMIT)


def _disc_kernel(lr_ref, li_ref, ls_ref, bre_ref, bim_ref, are_ref, aim_ref, br_ref, bi_ref):
    lr = lr_ref[...]
    li = li_ref[...]
    step = jnp.exp(ls_ref[...])
    decay = jnp.exp(lr * step)
    a_re = decay * jnp.cos(li * step)
    a_im = decay * jnp.sin(li * step)
    den = lr * lr + li * li
    f_re = ((a_re - 1.0) * lr + a_im * li) / den
    f_im = (a_im * lr - (a_re - 1.0) * li) / den
    b_re = bre_ref[...]
    b_im = bim_ref[...]
    are_ref[...] = a_re
    aim_ref[...] = a_im
    br_ref[...] = f_re * b_re - f_im * b_im
    bi_ref[...] = f_re * b_im + f_im * b_re


def _discretise(lam_re, lam_im, log_step, b_re, b_im):
    g, n = lam_re.shape
    k = b_re.shape[-1]
    gn = g * n
    col = lambda a: a.reshape(gn, 1)
    ls = jnp.broadcast_to(log_step[:, None], (g, n))
    outs = pl.pallas_call(
        _disc_kernel,
        out_shape=[jax.ShapeDtypeStruct((gn, 1), F32), jax.ShapeDtypeStruct((gn, 1), F32),
                   jax.ShapeDtypeStruct((gn, k), F32), jax.ShapeDtypeStruct((gn, k), F32)],
        name="s5_discretise",
    )(col(lam_re), col(lam_im), col(ls), b_re.reshape(gn, k), b_im.reshape(gn, k))
    a_re, a_im, br, bi = outs
    return a_re.reshape(g, n), a_im.reshape(g, n), br.reshape(g, n, k), bi.reshape(g, n, k)


def _block_diag_weights(br, bi, c_re, c_im):
    g, n, k = br.shape
    tiles = g // GROUPS_PER_TILE
    eye = jnp.eye(GROUPS_PER_TILE, dtype=F32)

    def bmat(b):
        b4 = b.reshape(tiles, GROUPS_PER_TILE, n, k)
        return jnp.einsum('qgnk,gh->qgkhn', b4, eye).reshape(tiles, CH_TILE, STATES_PER_TILE)

    def cmat(c):
        c4 = c.reshape(tiles, GROUPS_PER_TILE, k, n)
        return jnp.einsum('qgkn,gh->qgnhk', c4, eye).reshape(tiles, STATES_PER_TILE, CH_TILE)

    b_blk = jnp.concatenate([bmat(br), bmat(bi)], axis=-1).astype(BF16)
    c_blk = jnp.concatenate([cmat(c_re), -cmat(c_im)], axis=1).astype(BF16)
    return b_blk, c_blk


SCAN_LANES = 256


def _s5_kernel(x_ref, g_ref, win_ref, b_ref, are_ref, aim_ref, c_ref, d_ref, wglu_ref,
               o_ref, st_ref, bu_ref, z_ref, *, tc, nb, d_model):
    rows = tc * nb
    tiles = d_model // CH_TILE

    @pl.when(pl.program_id(0) == 0)
    def _():
        st_ref[...] = jnp.zeros_like(st_ref)

    x = x_ref[...].reshape(rows, d_model)
    h = _rms(x, g_ref[...]).astype(BF16)
    u = jnp.dot(h, win_ref[...], preferred_element_type=F32)
    u_bf = u.astype(BF16)

    for q in range(tiles):
        ch = slice(q * CH_TILE, (q + 1) * CH_TILE)
        bu_ref[...] = jnp.dot(u_bf[:, ch], b_ref[q], preferred_element_type=F32)

        for c in range(STATES_PER_TILE // SCAN_LANES):
            re = slice(c * SCAN_LANES, (c + 1) * SCAN_LANES)
            im = slice(STATES_PER_TILE + c * SCAN_LANES, STATES_PER_TILE + (c + 1) * SCAN_LANES)
            ar = jnp.broadcast_to(are_ref[q, :, re], (nb, SCAN_LANES))
            ai = jnp.broadcast_to(aim_ref[q, :, re], (nb, SCAN_LANES))

            def step(t, carry, re=re, im=im, ar=ar, ai=ai):
                sr, si = carry
                r0 = pl.multiple_of(t * nb, nb)
                nr = ar * sr - ai * si + bu_ref[pl.ds(r0, nb), re]
                ni = ar * si + ai * sr + bu_ref[pl.ds(r0, nb), im]
                bu_ref[pl.ds(r0, nb), re] = nr
                bu_ref[pl.ds(r0, nb), im] = ni
                return nr, ni

            sr, si = lax.fori_loop(0, tc, step, (st_ref[q, :, re], st_ref[q, :, im]), unroll=True)
            st_ref[q, :, re] = sr
            st_ref[q, :, im] = si

        y = jnp.dot(bu_ref[...].astype(BF16), c_ref[q], preferred_element_type=F32)
        y = y + d_ref[:, ch] * u[:, ch]
        z_ref[:, ch] = jax.nn.gelu(y).astype(BF16)

    glu = jnp.dot(z_ref[...], wglu_ref[...], preferred_element_type=F32)
    out = x + glu[:, :d_model] * jax.nn.sigmoid(glu[:, d_model:])
    o_ref[...] = out.reshape(tc, nb, d_model)


def _s5_layer(xt, gain, w_in, b_blk, a_re, a_im, c_blk, d_skip, w_glu, *, tc):
    seq, nb, d = xt.shape
    tiles = d // CH_TILE
    rows = tc * nb
    kern = functools.partial(_s5_kernel, tc=tc, nb=nb, d_model=d)
    return pl.pallas_call(
        kern,
        out_shape=jax.ShapeDtypeStruct(xt.shape, F32),
        grid=(seq // tc,),
        in_specs=[
            pl.BlockSpec((tc, nb, d), lambda i: (i, 0, 0)),
            _const_spec((1, d)),
            _const_spec((d, d)),
            _const_spec((tiles, CH_TILE, 2 * STATES_PER_TILE)),
            _const_spec((tiles, 1, STATES_PER_TILE)),
            _const_spec((tiles, 1, STATES_PER_TILE)),
            _const_spec((tiles, 2 * STATES_PER_TILE, CH_TILE)),
            _const_spec((1, d)),
            _const_spec((d, 2 * d)),
        ],
        out_specs=pl.BlockSpec((tc, nb, d), lambda i: (i, 0, 0)),
        scratch_shapes=[
            pltpu.VMEM((tiles, nb, 2 * STATES_PER_TILE), F32),
            pltpu.VMEM((rows, 2 * STATES_PER_TILE), F32),
            pltpu.VMEM((rows, d), BF16),
        ],
        compiler_params=_params(("arbitrary",)),
        name="s5_layer",
    )(xt, gain.reshape(1, d), w_in.astype(BF16), b_blk,
      a_re.reshape(tiles, 1, STATES_PER_TILE), a_im.reshape(tiles, 1, STATES_PER_TILE),
      c_blk, d_skip.reshape(1, d), w_glu.astype(BF16))


FF_CHUNK = MXU_TILE


def _ffn_body(x, g_ref, wgu_ref, wd_ref, d_ff):
    h = _rms(x, g_ref[...]).astype(BF16)
    acc = jnp.zeros(x.shape, F32)
    for c in range(d_ff // FF_CHUNK):
        gate = jnp.dot(h, wgu_ref[:, c * FF_CHUNK:(c + 1) * FF_CHUNK], preferred_element_type=F32)
        up = jnp.dot(h, wgu_ref[:, d_ff + c * FF_CHUNK:d_ff + (c + 1) * FF_CHUNK],
                     preferred_element_type=F32)
        act = (jax.nn.silu(gate) * up).astype(BF16)
        acc = acc + jnp.dot(act, wd_ref[c * FF_CHUNK:(c + 1) * FF_CHUNK, :],
                            preferred_element_type=F32)
    return x + acc


def _ffn_kernel(x_ref, g_ref, wgu_ref, wd_ref, o_ref, *, d_ff):
    o_ref[...] = _ffn_body(x_ref[...], g_ref, wgu_ref, wd_ref, d_ff)


def _proj_ffn_kernel(x_ref, a_ref, wo_ref, g_ref, wgu_ref, wd_ref, o_ref, *, d_ff):
    x = x_ref[...] + jnp.dot(a_ref[...], wo_ref[...], preferred_element_type=F32)
    o_ref[...] = _ffn_body(x, g_ref, wgu_ref, wd_ref, d_ff)


def _ffn_layer(x2, gain, w_gate_up, w_down, *, tm, attn=None, w_o=None):
    n, d = x2.shape
    d_ff = w_down.shape[0]
    row_spec = pl.BlockSpec((tm, d), lambda i: (i, 0))
    w_specs = [_const_spec((1, d)), _const_spec((d, 2 * d_ff)), _const_spec((d_ff, d))]
    w_args = (gain.reshape(1, d), w_gate_up.astype(BF16), w_down.astype(BF16))
    if attn is None:
        kern, in_specs, args = _ffn_kernel, [row_spec] + w_specs, (x2,) + w_args
    else:
        da = attn.shape[1]
        kern = _proj_ffn_kernel
        in_specs = [row_spec, pl.BlockSpec((tm, da), lambda i: (i, 0)), _const_spec((da, d))] + w_specs
        args = (x2, attn, w_o.astype(BF16)) + w_args
    return pl.pallas_call(
        functools.partial(kern, d_ff=d_ff),
        out_shape=jax.ShapeDtypeStruct((n, d), F32),
        grid=(n // tm,),
        in_specs=in_specs,
        out_specs=row_spec,
        compiler_params=_params(("parallel",)),
        name="ffn_layer",
    )(*args)


def _rope_tables(pos_ref, freq_ref, scale):
    ang = pos_ref[...].astype(F32) * freq_ref[...]
    lane = lax.broadcasted_iota(jnp.int32, ang.shape, 1)
    cos_t = jnp.cos(ang) * scale
    sin_t = jnp.sin(ang) * scale
    x1 = lane < QK_NOPE + HALF_ROPE
    return cos_t, jnp.where(x1, -sin_t, 0.0), jnp.where(x1, 0.0, sin_t)


def _rotate(tn, cos_t, sin_a, sin_b):
    return (tn * cos_t
            + pltpu.roll(tn, HEAD_TILE - HALF_ROPE, 1) * sin_a
            + pltpu.roll(tn, HALF_ROPE, 1) * sin_b)


def _segment_mean_matrix():
    idx = jnp.arange(HEAD_TILE)
    seg = jnp.where(idx < QK_NOPE, 0, jnp.where(idx < QK_DIM, 1, 2))
    same = (seg[:, None] == seg[None, :]) & (seg[:, None] < 2)
    width = jnp.where(seg == 0, QK_NOPE, QK_ROPE).astype(F32)
    return jnp.where(same, 1.0 / width[None, :], 0.0).astype(BF16)


def _head_tile_freq(inv_freq):
    z = jnp.zeros((QK_NOPE,), F32)
    pad = jnp.zeros((HEAD_TILE - QK_DIM,), F32)
    return jnp.concatenate([z, inv_freq, inv_freq, pad]).reshape(1, HEAD_TILE)


def _head_tile_gain(nope_gain, rope_gain):
    pad = jnp.zeros((HEAD_TILE - QK_DIM,), F32)
    return jnp.concatenate([nope_gain, rope_gain, pad]).reshape(1, HEAD_TILE)


def _kv_kernel(x_ref, pos_ref, freq_ref, g_ref, wa_ref, an_ref, wk_ref, wv_ref, e_ref, gt_ref,
               k_ref, v_ref):
    h = _rms(x_ref[...], g_ref[...]).astype(BF16)
    ckv = jnp.dot(h, wa_ref[...], preferred_element_type=F32)
    c_kv = _rms(ckv[:, :KV_LORA], an_ref[...]).astype(BF16)
    cos_t, sin_a, sin_b = _rope_tables(pos_ref, freq_ref, 1.0)
    gt = gt_ref[...]
    e = e_ref[...]

    kr = ckv[:, KV_LORA:]
    ms = jnp.dot((kr * kr).astype(BF16), e, preferred_element_type=F32)
    kr = _rotate(kr * lax.rsqrt(ms + EPS) * gt, cos_t, sin_a, sin_b)
    lane = lax.broadcasted_iota(jnp.int32, kr.shape, 1)

    kn = jnp.dot(c_kv, wk_ref[...], preferred_element_type=F32)
    for hd in range(N_HEADS):
        sl = slice(hd * HEAD_TILE, (hd + 1) * HEAD_TILE)
        t = kn[:, sl]
        ms = jnp.dot((t * t).astype(BF16), e, preferred_element_type=F32)
        tn = t * lax.rsqrt(ms + EPS) * gt
        k_ref[:, sl] = (jnp.where(lane < QK_NOPE, tn, kr)).astype(BF16)
    v_ref[...] = jnp.dot(c_kv, wv_ref[...], preferred_element_type=F32).astype(BF16)


def _kv_proj(x2, pos2, inv_freq, gain, w_kv_a, kv_a_norm, w_kv_b, k_nope_norm, k_rope_norm, *, tm):
    n, d = x2.shape
    wa_rope = jnp.zeros((d, HEAD_TILE), F32).at[:, QK_NOPE:QK_DIM].set(w_kv_a[:, KV_LORA:])
    wa = jnp.concatenate([w_kv_a[:, :KV_LORA], wa_rope], axis=1).astype(BF16)
    wb = w_kv_b.reshape(KV_LORA, N_HEADS, QK_NOPE + V_DIM)
    wk = jnp.zeros((KV_LORA, N_HEADS, HEAD_TILE), F32).at[:, :, :QK_NOPE].set(wb[:, :, :QK_NOPE])
    wk = wk.reshape(KV_LORA, N_HEADS * HEAD_TILE).astype(BF16)
    wv = wb[:, :, QK_NOPE:].reshape(KV_LORA, N_HEADS * V_DIM).astype(BF16)
    row = lambda w: pl.BlockSpec((tm, w), lambda i: (i, 0))
    return pl.pallas_call(
        _kv_kernel,
        out_shape=[jax.ShapeDtypeStruct((n, N_HEADS * HEAD_TILE), BF16),
                   jax.ShapeDtypeStruct((n, N_HEADS * V_DIM), BF16)],
        grid=(n // tm,),
        in_specs=[row(d), row(1), _const_spec((1, HEAD_TILE)), _const_spec((1, d)),
                  _const_spec(wa.shape), _const_spec((1, KV_LORA)), _const_spec(wk.shape),
                  _const_spec(wv.shape), _const_spec((HEAD_TILE, HEAD_TILE)),
                  _const_spec((1, HEAD_TILE))],
        out_specs=[row(N_HEADS * HEAD_TILE), row(N_HEADS * V_DIM)],
        compiler_params=_params(("parallel",)),
        name="kv_proj",
    )(x2, pos2, _head_tile_freq(inv_freq), gain.reshape(1, d), wa, kv_a_norm.reshape(1, KV_LORA),
      wk, wv, _segment_mean_matrix(), _head_tile_gain(k_nope_norm, k_rope_norm))


def _q_kernel(x_ref, pos_ref, freq_ref, g_ref, wa_ref, an_ref, wb_ref, e_ref, gt_ref, q_ref, *, scale):
    h = _rms(x_ref[...], g_ref[...]).astype(BF16)
    c_q = _rms(jnp.dot(h, wa_ref[...], preferred_element_type=F32), an_ref[...]).astype(BF16)
    q = jnp.dot(c_q, wb_ref[...], preferred_element_type=F32)
    cos_t, sin_a, sin_b = _rope_tables(pos_ref, freq_ref, scale)
    gt = gt_ref[...]
    e = e_ref[...]
    for hd in range(N_HEADS):
        sl = slice(hd * HEAD_TILE, (hd + 1) * HEAD_TILE)
        t = q[:, sl]
        ms = jnp.dot((t * t).astype(BF16), e, preferred_element_type=F32)
        tn = t * lax.rsqrt(ms + EPS) * gt
        q_ref[:, sl] = _rotate(tn, cos_t, sin_a, sin_b).astype(BF16)


def _q_proj(x2, pos2, inv_freq, gain, w_q_a, q_a_norm, w_q_b, q_nope_norm, q_rope_norm, *, tm):
    n, d = x2.shape
    rank = w_q_a.shape[1]
    wb = w_q_b.reshape(rank, N_HEADS, QK_DIM)
    wb = jnp.zeros((rank, N_HEADS, HEAD_TILE), F32).at[:, :, :QK_DIM].set(wb)
    wb = wb.reshape(rank, N_HEADS * HEAD_TILE).astype(BF16)
    row = lambda w: pl.BlockSpec((tm, w), lambda i: (i, 0))
    return pl.pallas_call(
        functools.partial(_q_kernel, scale=1.0 / math.sqrt(QK_DIM)),
        out_shape=jax.ShapeDtypeStruct((n, N_HEADS * HEAD_TILE), BF16),
        grid=(n // tm,),
        in_specs=[row(d), row(1), _const_spec((1, HEAD_TILE)), _const_spec((1, d)),
                  _const_spec((d, rank)), _const_spec((1, rank)), _const_spec(wb.shape),
                  _const_spec((HEAD_TILE, HEAD_TILE)), _const_spec((1, HEAD_TILE))],
        out_specs=row(N_HEADS * HEAD_TILE),
        compiler_params=_params(("parallel",)),
        name="q_proj",
    )(x2, pos2, _head_tile_freq(inv_freq), gain.reshape(1, d), w_q_a.astype(BF16),
      q_a_norm.reshape(1, rank), wb, _segment_mean_matrix(),
      _head_tile_gain(q_nope_norm, q_rope_norm))


def _attn_kernel(q_ref, k_ref, v_ref, o_ref, *, tq):
    i = pl.program_id(1)
    row = lax.broadcasted_iota(jnp.int32, (tq, tq), 0)
    col = lax.broadcasted_iota(jnp.int32, (tq, tq), 1)
    causal = col <= row
    lane = lax.broadcasted_iota(jnp.int32, (tq, 2 * V_DIM), 1)

    def head(hd, vcols):
        qh = q_ref[:, hd * HEAD_TILE:(hd + 1) * HEAD_TILE]

        def block(j, carry, masked):
            m, l, acc = carry
            k0 = pl.multiple_of(j * tq, tq)
            kh = k_ref[pl.ds(k0, tq), hd * HEAD_TILE:(hd + 1) * HEAD_TILE]
            s = lax.dot_general(qh, kh, (((1,), (1,)), ((), ())), preferred_element_type=F32)
            if masked:
                s = jnp.where(causal, s, NEG_INF)
            m_new = jnp.maximum(m, jnp.max(s, axis=-1, keepdims=True))
            alpha = jnp.exp(m - m_new)
            p = jnp.exp(s - m_new)
            l = alpha * l + jnp.sum(p, axis=-1, keepdims=True)
            pv = jnp.dot(p.astype(BF16), v_ref[pl.ds(k0, tq), vcols], preferred_element_type=F32)
            return m_new, l, alpha * acc + pv

        init = (jnp.full((tq, 1), NEG_INF, F32), jnp.zeros((tq, 1), F32),
                jnp.zeros((tq, 2 * V_DIM), F32))
        carry = lax.fori_loop(0, i, functools.partial(block, masked=False), init)
        _, l, acc = block(i, carry, True)
        return acc / l

    for pair in range(N_HEADS // 2):
        vcols = slice(pair * 2 * V_DIM, (pair + 1) * 2 * V_DIM)
        even = head(2 * pair, vcols)
        odd = head(2 * pair + 1, vcols)
        o_ref[:, vcols] = jnp.where(lane < V_DIM, even, odd).astype(BF16)


def _attention(q, k, v, *, tq):
    bsz, seq, qw = q.shape
    vw = v.shape[-1]
    return pl.pallas_call(
        functools.partial(_attn_kernel, tq=tq),
        out_shape=jax.ShapeDtypeStruct((bsz, seq, vw), BF16),
        grid=(bsz, seq // tq),
        in_specs=[pl.BlockSpec((None, tq, qw), lambda b, i: (b, i, 0)),
                  pl.BlockSpec((None, seq, qw), lambda b, i: (b, 0, 0)),
                  pl.BlockSpec((None, seq, vw), lambda b, i: (b, 0, 0))],
        out_specs=pl.BlockSpec((None, tq, vw), lambda b, i: (b, i, 0)),
        compiler_params=_params(("parallel", "arbitrary")),
        name="causal_attention",
    )(q, k, v)


def _tiles(bsz, seq):
    n = bsz * seq
    tm = 512 if n % 512 == 0 else n
    tc = 16 if seq % 16 == 0 else seq
    tq = 256 if seq % 256 == 0 else seq
    return tm, tc, tq


def kernel(x, positions, mix_norm, ffn_norm, ffn_w_gate_up, ffn_w_down, ssm_w_in, ssm_lambda_re, ssm_lambda_im, ssm_log_step, ssm_b_re, ssm_b_im, ssm_c_re, ssm_c_im, ssm_d, ssm_w_glu, kv_in_norm, mla_w_kv_a, mla_kv_a_norm, mla_w_kv_b, mla_k_nope_norm, mla_k_rope_norm, mla_w_q_a, mla_q_a_norm, mla_w_q_b, mla_q_nope_norm, mla_q_rope_norm, mla_w_o):
    bsz, seq, d = x.shape
    n = bsz * seq
    depth = mix_norm.shape[0]
    n_ssm = ssm_w_in.shape[0]
    assert bsz % 8 == 0 and d % CH_TILE == 0
    tm, tc, tq = _tiles(bsz, seq)

    xt = jnp.transpose(x, (1, 0, 2))
    for i in range(n_ssm):
        a_re, a_im, br, bi = _discretise(ssm_lambda_re[i], ssm_lambda_im[i], ssm_log_step[i],
                                         ssm_b_re[i], ssm_b_im[i])
        b_blk, c_blk = _block_diag_weights(br, bi, ssm_c_re[i], ssm_c_im[i])
        xt = _s5_layer(xt, mix_norm[i], ssm_w_in[i], b_blk, a_re, a_im, c_blk, ssm_d[i],
                       ssm_w_glu[i], tc=tc)
        xt = _ffn_layer(xt.reshape(n, d), ffn_norm[i], ffn_w_gate_up[i], ffn_w_down[i],
                        tm=tm).reshape(seq, bsz, d)
    x2 = jnp.transpose(xt, (1, 0, 2)).reshape(n, d)

    pos2 = positions.reshape(n, 1)
    inv_freq = ROPE_THETA ** (-jnp.arange(0, QK_ROPE, 2, dtype=F32) / QK_ROPE)
    k, v = _kv_proj(x2, pos2, inv_freq, kv_in_norm, mla_w_kv_a, mla_kv_a_norm, mla_w_kv_b,
                    mla_k_nope_norm, mla_k_rope_norm, tm=tm)
    k = k.reshape(bsz, seq, -1)
    v = v.reshape(bsz, seq, -1)
    for layer in range(n_ssm, depth):
        j = layer - n_ssm
        q = _q_proj(x2, pos2, inv_freq, mix_norm[layer], mla_w_q_a[j], mla_q_a_norm[j],
                    mla_w_q_b[j], mla_q_nope_norm[j], mla_q_rope_norm[j], tm=tm)
        o = _attention(q.reshape(bsz, seq, -1), k, v, tq=tq)
        x2 = _ffn_layer(x2, ffn_norm[layer], ffn_w_gate_up[layer], ffn_w_down[layer], tm=tm,
                        attn=o.reshape(n, -1), w_o=mla_w_o[j])
    return x2.reshape(bsz, seq, d)
```

```python
import functools
import math

import jax
import jax.numpy as jnp
from jax import lax
from jax.experimental import pallas as pl
from jax.experimental.pallas import tpu as pltpu

F32 = jnp.float32
BF16 = jnp.bfloat16

EPS = 1e-6
NEG_INF = -1e30
ROPE_THETA = 10000.0

SSM_GROUP = 16
SSM_STATE = 64
N_HEADS = 16
QK_NOPE = 64
QK_ROPE = 32
HALF_ROPE = QK_ROPE // 2
QK_DIM = QK_NOPE + QK_ROPE
V_DIM = 64
KV_LORA = 256

LANES = 128
MXU_TILE = 256
VMEM_LIMIT = 56 * 1024 * 1024

HEAD_TILE = LANES
CH_TILE = MXU_TILE
GROUPS_PER_TILE = CH_TILE // SSM_GROUP
STATES_PER_TILE = GROUPS_PER_TILE * SSM_STATE


def _rms(x, gain):
    ms = jnp.mean(x * x, axis=-1, keepdims=True)
    return x * lax.rsqrt(ms + EPS) * gain


def _const_spec(shape):
    nd = len(shape)
    return pl.BlockSpec(shape, lambda *_: (0,) * nd, pipeline_mode=pl.Buffered(1))


def _params(semantics):
    return pltpu.CompilerParams(dimension_semantics=semantics, vmem_limit_bytes=VMEM_LIMIT)


def _disc_kernel(lr_ref, li_ref, ls_ref, bre_ref, bim_ref, are_ref, aim_ref, br_ref, bi_ref):
    lr = lr_ref[...]
    li = li_ref[...]
    step = jnp.exp(ls_ref[...])
    decay = jnp.exp(lr * step)
    a_re = decay * jnp.cos(li * step)
    a_im = decay * jnp.sin(li * step)
    den = lr * lr + li * li
    f_re = ((a_re - 1.0) * lr + a_im * li) / den
    f_im = (a_im * lr - (a_re - 1.0) * li) / den
    b_re = bre_ref[...]
    b_im = bim_ref[...]
    are_ref[...] = a_re
    aim_ref[...] = a_im
    br_ref[...] = f_re * b_re - f_im * b_im
    bi_ref[...] = f_re * b_im + f_im * b_re


def _discretise(lam_re, lam_im, log_step, b_re, b_im):
    g, n = lam_re.shape
    k = b_re.shape[-1]
    gn = g * n
    col = lambda a: a.reshape(gn, 1)
    ls = jnp.broadcast_to(log_step[:, None], (g, n))
    outs = pl.pallas_call(
        _disc_kernel,
        out_shape=[jax.ShapeDtypeStruct((gn, 1), F32), jax.ShapeDtypeStruct((gn, 1), F32),
                   jax.ShapeDtypeStruct((gn, k), F32), jax.ShapeDtypeStruct((gn, k), F32)],
        name="s5_discretise",
    )(col(lam_re), col(lam_im), col(ls), b_re.reshape(gn, k), b_im.reshape(gn, k))
    a_re, a_im, br, bi = outs
    return a_re.reshape(g, n), a_im.reshape(g, n), br.reshape(g, n, k), bi.reshape(g, n, k)


def _block_diag_weights(br, bi, c_re, c_im):
    g, n, k = br.shape
    tiles = g // GROUPS_PER_TILE
    eye = jnp.eye(GROUPS_PER_TILE, dtype=F32)

    def bmat(b):
        b4 = b.reshape(tiles, GROUPS_PER_TILE, n, k)
        return jnp.einsum('qgnk,gh->qgkhn', b4, eye).reshape(tiles, CH_TILE, STATES_PER_TILE)

    def cmat(c):
        c4 = c.reshape(tiles, GROUPS_PER_TILE, k, n)
        return jnp.einsum('qgkn,gh->qgnhk', c4, eye).reshape(tiles, STATES_PER_TILE, CH_TILE)

    b_blk = jnp.concatenate([bmat(br), bmat(bi)], axis=-1).astype(BF16)
    c_blk = jnp.concatenate([cmat(c_re), -cmat(c_im)], axis=1).astype(BF16)
    return b_blk, c_blk


SCAN_LANES = 256


def _s5_kernel(x_ref, g_ref, win_ref, b_ref, are_ref, aim_ref, c_ref, d_ref, wglu_ref,
               o_ref, st_ref, bu_ref, z_ref, *, tc, nb, d_model):
    rows = tc * nb
    tiles = d_model // CH_TILE

    @pl.when(pl.program_id(0) == 0)
    def _():
        st_ref[...] = jnp.zeros_like(st_ref)

    x = x_ref[...].reshape(rows, d_model)
    h = _rms(x, g_ref[...]).astype(BF16)
    u = jnp.dot(h, win_ref[...], preferred_element_type=F32)
    u_bf = u.astype(BF16)

    for q in range(tiles):
        ch = slice(q * CH_TILE, (q + 1) * CH_TILE)
        bu_ref[...] = jnp.dot(u_bf[:, ch], b_ref[q], preferred_element_type=F32)

        for c in range(STATES_PER_TILE // SCAN_LANES):
            re = slice(c * SCAN_LANES, (c + 1) * SCAN_LANES)
            im = slice(STATES_PER_TILE + c * SCAN_LANES, STATES_PER_TILE + (c + 1) * SCAN_LANES)
            ar = jnp.broadcast_to(are_ref[q, :, re], (nb, SCAN_LANES))
            ai = jnp.broadcast_to(aim_ref[q, :, re], (nb, SCAN_LANES))

            def step(t, carry, re=re, im=im, ar=ar, ai=ai):
                sr, si = carry
                r0 = pl.multiple_of(t * nb, nb)
                nr = ar * sr - ai * si + bu_ref[pl.ds(r0, nb), re]
                ni = ar * si + ai * sr + bu_ref[pl.ds(r0, nb), im]
                bu_ref[pl.ds(r0, nb), re] = nr
                bu_ref[pl.ds(r0, nb), im] = ni
                return nr, ni

            sr, si = lax.fori_loop(0, tc, step, (st_ref[q, :, re], st_ref[q, :, im]), unroll=True)
            st_ref[q, :, re] = sr
            st_ref[q, :, im] = si

        y = jnp.dot(bu_ref[...].astype(BF16), c_ref[q], preferred_element_type=F32)
        y = y + d_ref[:, ch] * u[:, ch]
        z_ref[:, ch] = jax.nn.gelu(y).astype(BF16)

    glu = jnp.dot(z_ref[...], wglu_ref[...], preferred_element_type=F32)
    out = x + glu[:, :d_model] * jax.nn.sigmoid(glu[:, d_model:])
    o_ref[...] = out.reshape(tc, nb, d_model)


def _s5_layer(xt, gain, w_in, b_blk, a_re, a_im, c_blk, d_skip, w_glu, *, tc):
    seq, nb, d = xt.shape
    tiles = d // CH_TILE
    rows = tc * nb
    kern = functools.partial(_s5_kernel, tc=tc, nb=nb, d_model=d)
    return pl.pallas_call(
        kern,
        out_shape=jax.ShapeDtypeStruct(xt.shape, F32),
        grid=(seq // tc,),
        in_specs=[
            pl.BlockSpec((tc, nb, d), lambda i: (i, 0, 0)),
            _const_spec((1, d)),
            _const_spec((d, d)),
            _const_spec((tiles, CH_TILE, 2 * STATES_PER_TILE)),
            _const_spec((tiles, 1, STATES_PER_TILE)),
            _const_spec((tiles, 1, STATES_PER_TILE)),
            _const_spec((tiles, 2 * STATES_PER_TILE, CH_TILE)),
            _const_spec((1, d)),
            _const_spec((d, 2 * d)),
        ],
        out_specs=pl.BlockSpec((tc, nb, d), lambda i: (i, 0, 0)),
        scratch_shapes=[
            pltpu.VMEM((tiles, nb, 2 * STATES_PER_TILE), F32),
            pltpu.VMEM((rows, 2 * STATES_PER_TILE), F32),
            pltpu.VMEM((rows, d), BF16),
        ],
        compiler_params=_params(("arbitrary",)),
        name="s5_layer",
    )(xt, gain.reshape(1, d), w_in.astype(BF16), b_blk,
      a_re.reshape(tiles, 1, STATES_PER_TILE), a_im.reshape(tiles, 1, STATES_PER_TILE),
      c_blk, d_skip.reshape(1, d), w_glu.astype(BF16))


FF_CHUNK = MXU_TILE


def _ffn_body(x, g_ref, wgu_ref, wd_ref, d_ff):
    h = _rms(x, g_ref[...]).astype(BF16)
    acc = jnp.zeros(x.shape, F32)
    for c in range(d_ff // FF_CHUNK):
        gate = jnp.dot(h, wgu_ref[:, c * FF_CHUNK:(c + 1) * FF_CHUNK], preferred_element_type=F32)
        up = jnp.dot(h, wgu_ref[:, d_ff + c * FF_CHUNK:d_ff + (c + 1) * FF_CHUNK],
                     preferred_element_type=F32)
        act = (jax.nn.silu(gate) * up).astype(BF16)
        acc = acc + jnp.dot(act, wd_ref[c * FF_CHUNK:(c + 1) * FF_CHUNK, :],
                            preferred_element_type=F32)
    return x + acc


def _ffn_kernel(x_ref, g_ref, wgu_ref, wd_ref, o_ref, *, d_ff):
    o_ref[...] = _ffn_body(x_ref[...], g_ref, wgu_ref, wd_ref, d_ff)


def _proj_ffn_kernel(x_ref, a_ref, wo_ref, g_ref, wgu_ref, wd_ref, o_ref, *, d_ff):
    attn = jnp.concatenate([a_ref[p] for p in range(a_ref.shape[0])], axis=-1)
    x = x_ref[...] + jnp.dot(attn, wo_ref[...], preferred_element_type=F32)
    o_ref[...] = _ffn_body(x, g_ref, wgu_ref, wd_ref, d_ff)


def _ffn_layer(x2, gain, w_gate_up, w_down, *, tm, attn=None, w_o=None):
    n, d = x2.shape
    d_ff = w_down.shape[0]
    row_spec = pl.BlockSpec((tm, d), lambda i: (i, 0))
    w_specs = [_const_spec((1, d)), _const_spec((d, 2 * d_ff)), _const_spec((d_ff, d))]
    w_args = (gain.reshape(1, d), w_gate_up.astype(BF16), w_down.astype(BF16))
    if attn is None:
        kern, in_specs, args = _ffn_kernel, [row_spec] + w_specs, (x2,) + w_args
    else:
        _, pairs, seq, width = attn.shape
        per_b = seq // tm
        kern = _proj_ffn_kernel
        attn_spec = pl.BlockSpec((None, pairs, tm, width), lambda i: (i // per_b, 0, i % per_b, 0))
        in_specs = [row_spec, attn_spec, _const_spec((pairs * width, d))] + w_specs
        args = (x2, attn, w_o.astype(BF16)) + w_args
    return pl.pallas_call(
        functools.partial(kern, d_ff=d_ff),
        out_shape=jax.ShapeDtypeStruct((n, d), F32),
        grid=(n // tm,),
        in_specs=in_specs,
        out_specs=row_spec,
        compiler_params=_params(("parallel",)),
        name="ffn_layer",
    )(*args)


def _rope_tables(pos_ref, freq_ref, scale):
    ang = pos_ref[...].astype(F32) * freq_ref[...]
    lane = lax.broadcasted_iota(jnp.int32, ang.shape, 1)
    cos_t = jnp.cos(ang) * scale
    sin_t = jnp.sin(ang) * scale
    x1 = lane < QK_NOPE + HALF_ROPE
    return cos_t, jnp.where(x1, -sin_t, 0.0), jnp.where(x1, 0.0, sin_t)


def _rotate(tn, cos_t, sin_a, sin_b):
    return (tn * cos_t
            + pltpu.roll(tn, HEAD_TILE - HALF_ROPE, 1) * sin_a
            + pltpu.roll(tn, HALF_ROPE, 1) * sin_b)


def _segment_mean_matrix():
    idx = jnp.arange(HEAD_TILE)
    seg = jnp.where(idx < QK_NOPE, 0, jnp.where(idx < QK_DIM, 1, 2))
    same = (seg[:, None] == seg[None, :]) & (seg[:, None] < 2)
    width = jnp.where(seg == 0, QK_NOPE, QK_ROPE).astype(F32)
    return jnp.where(same, 1.0 / width[None, :], 0.0).astype(BF16)


def _head_tile_freq(inv_freq):
    z = jnp.zeros((QK_NOPE,), F32)
    pad = jnp.zeros((HEAD_TILE - QK_DIM,), F32)
    return jnp.concatenate([z, inv_freq, inv_freq, pad]).reshape(1, HEAD_TILE)


def _head_tile_gain(nope_gain, rope_gain):
    pad = jnp.zeros((HEAD_TILE - QK_DIM,), F32)
    return jnp.concatenate([nope_gain, rope_gain, pad]).reshape(1, HEAD_TILE)


def _kv_kernel(x_ref, pos_ref, freq_ref, g_ref, wa_ref, an_ref, wk_ref, wv_ref, e_ref, gt_ref,
               k_ref, v_ref):
    h = _rms(x_ref[...], g_ref[...]).astype(BF16)
    ckv = jnp.dot(h, wa_ref[...], preferred_element_type=F32)
    c_kv = _rms(ckv[:, :KV_LORA], an_ref[...]).astype(BF16)
    cos_t, sin_a, sin_b = _rope_tables(pos_ref, freq_ref, 1.0)
    gt = gt_ref[...]
    e = e_ref[...]

    kr = ckv[:, KV_LORA:]
    ms = jnp.dot((kr * kr).astype(BF16), e, preferred_element_type=F32)
    kr = _rotate(kr * lax.rsqrt(ms + EPS) * gt, cos_t, sin_a, sin_b)
    lane = lax.broadcasted_iota(jnp.int32, kr.shape, 1)

    kn = jnp.dot(c_kv, wk_ref[...], preferred_element_type=F32)
    for hd in range(N_HEADS):
        sl = slice(hd * HEAD_TILE, (hd + 1) * HEAD_TILE)
        t = kn[:, sl]
        ms = jnp.dot((t * t).astype(BF16), e, preferred_element_type=F32)
        tn = t * lax.rsqrt(ms + EPS) * gt
        k_ref[hd] = (jnp.where(lane < QK_NOPE, tn, kr)).astype(BF16)
    v = jnp.dot(c_kv, wv_ref[...], preferred_element_type=F32).astype(BF16)
    for p in range(N_HEADS // 2):
        v_ref[p] = v[:, p * HEAD_TILE:(p + 1) * HEAD_TILE]


def _head_major_spec(heads, tm, seq):
    per_b = seq // tm
    return pl.BlockSpec((None, heads, tm, HEAD_TILE), lambda i: (i // per_b, 0, i % per_b, 0))


def _kv_proj(x2, pos2, inv_freq, gain, w_kv_a, kv_a_norm, w_kv_b, k_nope_norm, k_rope_norm, *,
             tm, bsz):
    n, d = x2.shape
    seq = n // bsz
    wa_rope = jnp.zeros((d, HEAD_TILE), F32).at[:, QK_NOPE:QK_DIM].set(w_kv_a[:, KV_LORA:])
    wa = jnp.concatenate([w_kv_a[:, :KV_LORA], wa_rope], axis=1).astype(BF16)
    wb = w_kv_b.reshape(KV_LORA, N_HEADS, QK_NOPE + V_DIM)
    wk = jnp.zeros((KV_LORA, N_HEADS, HEAD_TILE), F32).at[:, :, :QK_NOPE].set(wb[:, :, :QK_NOPE])
    wk = wk.reshape(KV_LORA, N_HEADS * HEAD_TILE).astype(BF16)
    wv = wb[:, :, QK_NOPE:].reshape(KV_LORA, N_HEADS * V_DIM).astype(BF16)
    row = lambda w: pl.BlockSpec((tm, w), lambda i: (i, 0))
    return pl.pallas_call(
        _kv_kernel,
        out_shape=[jax.ShapeDtypeStruct((bsz, N_HEADS, seq, HEAD_TILE), BF16),
                   jax.ShapeDtypeStruct((bsz, N_HEADS // 2, seq, HEAD_TILE), BF16)],
        grid=(n // tm,),
        in_specs=[row(d), row(1), _const_spec((1, HEAD_TILE)), _const_spec((1, d)),
                  _const_spec(wa.shape), _const_spec((1, KV_LORA)), _const_spec(wk.shape),
                  _const_spec(wv.shape), _const_spec((HEAD_TILE, HEAD_TILE)),
                  _const_spec((1, HEAD_TILE))],
        out_specs=[_head_major_spec(N_HEADS, tm, seq), _head_major_spec(N_HEADS // 2, tm, seq)],
        compiler_params=_params(("parallel",)),
        name="kv_proj",
    )(x2, pos2, _head_tile_freq(inv_freq), gain.reshape(1, d), wa, kv_a_norm.reshape(1, KV_LORA),
      wk, wv, _segment_mean_matrix(), _head_tile_gain(k_nope_norm, k_rope_norm))


def _q_kernel(x_ref, pos_ref, freq_ref, g_ref, wa_ref, an_ref, wb_ref, e_ref, gt_ref, q_ref, *, scale):
    h = _rms(x_ref[...], g_ref[...]).astype(BF16)
    c_q = _rms(jnp.dot(h, wa_ref[...], preferred_element_type=F32), an_ref[...]).astype(BF16)
    q = jnp.dot(c_q, wb_ref[...], preferred_element_type=F32)
    cos_t, sin_a, sin_b = _rope_tables(pos_ref, freq_ref, scale)
    gt = gt_ref[...]
    e = e_ref[...]
    for hd in range(N_HEADS):
        sl = slice(hd * HEAD_TILE, (hd + 1) * HEAD_TILE)
        t = q[:, sl]
        ms = jnp.dot((t * t).astype(BF16), e, preferred_element_type=F32)
        tn = t * lax.rsqrt(ms + EPS) * gt
        q_ref[hd] = _rotate(tn, cos_t, sin_a, sin_b).astype(BF16)


def _q_proj(x2, pos2, inv_freq, gain, w_q_a, q_a_norm, w_q_b, q_nope_norm, q_rope_norm, *, tm, bsz):
    n, d = x2.shape
    seq = n // bsz
    rank = w_q_a.shape[1]
    wb = w_q_b.reshape(rank, N_HEADS, QK_DIM)
    wb = jnp.zeros((rank, N_HEADS, HEAD_TILE), F32).at[:, :, :QK_DIM].set(wb)
    wb = wb.reshape(rank, N_HEADS * HEAD_TILE).astype(BF16)
    row = lambda w: pl.BlockSpec((tm, w), lambda i: (i, 0))
    return pl.pallas_call(
        functools.partial(_q_kernel, scale=math.log2(math.e) / math.sqrt(QK_DIM)),
        out_shape=jax.ShapeDtypeStruct((bsz, N_HEADS, seq, HEAD_TILE), BF16),
        grid=(n // tm,),
        in_specs=[row(d), row(1), _const_spec((1, HEAD_TILE)), _const_spec((1, d)),
                  _const_spec((d, rank)), _const_spec((1, rank)), _const_spec(wb.shape),
                  _const_spec((HEAD_TILE, HEAD_TILE)), _const_spec((1, HEAD_TILE))],
        out_specs=_head_major_spec(N_HEADS, tm, seq),
        compiler_params=_params(("parallel",)),
        name="q_proj",
    )(x2, pos2, _head_tile_freq(inv_freq), gain.reshape(1, d), w_q_a.astype(BF16),
      q_a_norm.reshape(1, rank), wb, _segment_mean_matrix(),
      _head_tile_gain(q_nope_norm, q_rope_norm))


def _attn_kernel(q_ref, k_ref, v_ref, o_ref, s_ref, *, tq, nq):
    i = pl.program_id(1)
    row = lax.broadcasted_iota(jnp.int32, (tq, tq), 0)
    col = lax.broadcasted_iota(jnp.int32, (tq, tq), 1)
    causal = col <= row
    lane = lax.broadcasted_iota(jnp.int32, (tq, HEAD_TILE), 1)

    def fold(a, op):
        out = a[:, :LANES]
        for t in range(1, tq // LANES):
            out = op(out, a[:, t * LANES:(t + 1) * LANES])
        return out

    def one_head(h, hp, par, nk):
        qh = q_ref[h]
        mx = None
        for j in range(nk):
            s = lax.dot_general(qh, k_ref[h, j * tq:(j + 1) * tq, :], (((1,), (1,)), ((), ())),
                                preferred_element_type=F32)
            if j == nk - 1:
                s = jnp.where(causal, s, NEG_INF)
            s_ref[par, :, j * tq:(j + 1) * tq] = s
            f = fold(s, jnp.maximum)
            mx = f if mx is None else jnp.maximum(mx, f)
        m = jnp.max(mx, axis=-1, keepdims=True)
        ls = jnp.zeros((tq, LANES), F32)
        acc = jnp.zeros((tq, HEAD_TILE), F32)
        for j in range(nk):
            p = jnp.exp2(s_ref[par, :, j * tq:(j + 1) * tq] - m)
            ls = ls + fold(p, jnp.add)
            acc = acc + jnp.dot(p.astype(BF16), v_ref[hp, j * tq:(j + 1) * tq, :],
                                preferred_element_type=F32)
        return acc / jnp.sum(ls, axis=-1, keepdims=True)

    for c in range(nq):
        @pl.when(i == c)
        def _(c=c):
            def pair(hp, carry):
                even = one_head(2 * hp, hp, 0, c + 1)
                odd = one_head(2 * hp + 1, hp, 1, c + 1)
                o_ref[hp] = jnp.where(lane < V_DIM, even, odd).astype(BF16)
                return carry
            lax.fori_loop(0, N_HEADS // 2, pair, 0)


def _attention(q, k, v, *, tq):
    bsz, nh, seq, _ = q.shape
    nq = seq // tq
    return pl.pallas_call(
        functools.partial(_attn_kernel, tq=tq, nq=nq),
        out_shape=jax.ShapeDtypeStruct(v.shape, BF16),
        grid=(bsz, nq),
        in_specs=[pl.BlockSpec((None, nh, tq, HEAD_TILE), lambda b, i: (b, 0, i, 0)),
                  pl.BlockSpec((None, nh, seq, HEAD_TILE), lambda b, i: (b, 0, 0, 0)),
                  pl.BlockSpec((None, nh // 2, seq, HEAD_TILE), lambda b, i: (b, 0, 0, 0))],
        out_specs=pl.BlockSpec((None, nh // 2, tq, HEAD_TILE), lambda b, i: (b, 0, i, 0)),
        scratch_shapes=[pltpu.VMEM((2, tq, seq), F32)],
        compiler_params=_params(("parallel", "arbitrary")),
        name="causal_attention",
    )(q, k, v)


def _tiles(bsz, seq):
    tm = 512 if seq % 512 == 0 else seq
    tc = 16 if seq % 16 == 0 else seq
    tq = 256 if seq % 256 == 0 else seq
    return tm, tc, tq


def kernel(x, positions, mix_norm, ffn_norm, ffn_w_gate_up, ffn_w_down, ssm_w_in, ssm_lambda_re, ssm_lambda_im, ssm_log_step, ssm_b_re, ssm_b_im, ssm_c_re, ssm_c_im, ssm_d, ssm_w_glu, kv_in_norm, mla_w_kv_a, mla_kv_a_norm, mla_w_kv_b, mla_k_nope_norm, mla_k_rope_norm, mla_w_q_a, mla_q_a_norm, mla_w_q_b, mla_q_nope_norm, mla_q_rope_norm, mla_w_o):
    bsz, seq, d = x.shape
    n = bsz * seq
    depth = mix_norm.shape[0]
    n_ssm = ssm_w_in.shape[0]
    assert bsz % 8 == 0 and d % CH_TILE == 0
    tm, tc, tq = _tiles(bsz, seq)

    xt = jnp.transpose(x, (1, 0, 2))
    for i in range(n_ssm):
        a_re, a_im, br, bi = _discretise(ssm_lambda_re[i], ssm_lambda_im[i], ssm_log_step[i],
                                         ssm_b_re[i], ssm_b_im[i])
        b_blk, c_blk = _block_diag_weights(br, bi, ssm_c_re[i], ssm_c_im[i])
        xt = _s5_layer(xt, mix_norm[i], ssm_w_in[i], b_blk, a_re, a_im, c_blk, ssm_d[i],
                       ssm_w_glu[i], tc=tc)
        xt = _ffn_layer(xt.reshape(n, d), ffn_norm[i], ffn_w_gate_up[i], ffn_w_down[i],
                        tm=tm).reshape(seq, bsz, d)
    x2 = jnp.transpose(xt, (1, 0, 2)).reshape(n, d)

    pos2 = positions.reshape(n, 1)
    inv_freq = ROPE_THETA ** (-jnp.arange(0, QK_ROPE, 2, dtype=F32) / QK_ROPE)
    k, v = _kv_proj(x2, pos2, inv_freq, kv_in_norm, mla_w_kv_a, mla_kv_a_norm, mla_w_kv_b,
                    mla_k_nope_norm, mla_k_rope_norm, tm=tm, bsz=bsz)
    for layer in range(n_ssm, depth):
        j = layer - n_ssm
        q = _q_proj(x2, pos2, inv_freq, mix_norm[layer], mla_w_q_a[j], mla_q_a_norm[j],
                    mla_w_q_b[j], mla_q_nope_norm[j], mla_q_rope_norm[j], tm=tm, bsz=bsz)
        o = _attention(q, k, v, tq=tq)
        x2 = _ffn_layer(x2, ffn_norm[layer], ffn_w_gate_up[layer], ffn_w_down[layer], tm=tm,
                        attn=o, w_o=mla_w_o[j])
    return x2.reshape(bsz, seq, d)
```

```python
import functools
import math

import jax
import jax.numpy as jnp
from jax import lax
from jax.experimental import pallas as pl
from jax.experimental.pallas import tpu as pltpu

F32 = jnp.float32
BF16 = jnp.bfloat16

EPS = 1e-6
NEG_INF = -1e30
ROPE_THETA = 10000.0

SSM_GROUP = 16
SSM_STATE = 64
N_HEADS = 16
QK_NOPE = 64
QK_ROPE = 32
HALF_ROPE = QK_ROPE // 2
QK_DIM = QK_NOPE + QK_ROPE
V_DIM = 64
KV_LORA = 256

LANES = 128
MXU_TILE = 256
VMEM_LIMIT = 56 * 1024 * 1024

HEAD_TILE = LANES
CH_TILE = MXU_TILE
GROUPS_PER_TILE = CH_TILE // SSM_GROUP
STATES_PER_TILE = GROUPS_PER_TILE * SSM_STATE


def _rms(x, gain):
    ms = jnp.mean(x * x, axis=-1, keepdims=True)
    return x * lax.rsqrt(ms + EPS) * gain


def _const_spec(shape):
    nd = len(shape)
    return pl.BlockSpec(shape, lambda *_: (0,) * nd, pipeline_mode=pl.Buffered(1))


def _params(semantics):
    return pltpu.CompilerParams(dimension_semantics=semantics, vmem_limit_bytes=VMEM_LIMIT)


def _disc_kernel(lr_ref, li_ref, ls_ref, bre_ref, bim_ref, are_ref, aim_ref, br_ref, bi_ref):
    lr = lr_ref[...]
    li = li_ref[...]
    step = jnp.exp(ls_ref[...])
    decay = jnp.exp(lr * step)
    a_re = decay * jnp.cos(li * step)
    a_im = decay * jnp.sin(li * step)
    den = lr * lr + li * li
    f_re = ((a_re - 1.0) * lr + a_im * li) / den
    f_im = (a_im * lr - (a_re - 1.0) * li) / den
    b_re = bre_ref[...]
    b_im = bim_ref[...]
    are_ref[...] = a_re
    aim_ref[...] = a_im
    br_ref[...] = f_re * b_re - f_im * b_im
    bi_ref[...] = f_re * b_im + f_im * b_re


def _discretise(lam_re, lam_im, log_step, b_re, b_im):
    g, n = lam_re.shape
    k = b_re.shape[-1]
    gn = g * n
    col = lambda a: a.reshape(gn, 1)
    ls = jnp.broadcast_to(log_step[:, None], (g, n))
    outs = pl.pallas_call(
        _disc_kernel,
        out_shape=[jax.ShapeDtypeStruct((gn, 1), F32), jax.ShapeDtypeStruct((gn, 1), F32),
                   jax.ShapeDtypeStruct((gn, k), F32), jax.ShapeDtypeStruct((gn, k), F32)],
        name="s5_discretise",
    )(col(lam_re), col(lam_im), col(ls), b_re.reshape(gn, k), b_im.reshape(gn, k))
    a_re, a_im, br, bi = outs
    return a_re.reshape(g, n), a_im.reshape(g, n), br.reshape(g, n, k), bi.reshape(g, n, k)


def _block_diag_weights(br, bi, c_re, c_im):
    g, n, k = br.shape
    tiles = g // GROUPS_PER_TILE
    eye = jnp.eye(GROUPS_PER_TILE, dtype=F32)

    def bmat(b):
        b4 = b.reshape(tiles, GROUPS_PER_TILE, n, k)
        return jnp.einsum('qgnk,gh->qgkhn', b4, eye).reshape(tiles, CH_TILE, STATES_PER_TILE)

    def cmat(c):
        c4 = c.reshape(tiles, GROUPS_PER_TILE, k, n)
        return jnp.einsum('qgkn,gh->qgnhk', c4, eye).reshape(tiles, STATES_PER_TILE, CH_TILE)

    b_blk = jnp.concatenate([bmat(br), bmat(bi)], axis=-1).astype(BF16)
    c_blk = jnp.concatenate([cmat(c_re), -cmat(c_im)], axis=1).astype(BF16)
    return b_blk, c_blk


SCAN_LANES = 256


def _s5_kernel(x_ref, g_ref, win_ref, b_ref, are_ref, aim_ref, c_ref, d_ref, wglu_ref,
               o_ref, st_ref, bu_ref, z_ref, *, tc, nb, d_model):
    rows = tc * nb
    tiles = d_model // CH_TILE

    @pl.when(pl.program_id(0) == 0)
    def _():
        st_ref[...] = jnp.zeros_like(st_ref)

    x = x_ref[...].reshape(rows, d_model)
    h = _rms(x, g_ref[...]).astype(BF16)
    u = jnp.dot(h, win_ref[...], preferred_element_type=F32)
    u_bf = u.astype(BF16)

    for q in range(tiles):
        ch = slice(q * CH_TILE, (q + 1) * CH_TILE)
        bu = bu_ref
        bu[...] = jnp.dot(u_bf[:, ch], b_ref[q], preferred_element_type=F32)

        for c in range(STATES_PER_TILE // SCAN_LANES):
            re = slice(c * SCAN_LANES, (c + 1) * SCAN_LANES)
            im = slice(STATES_PER_TILE + c * SCAN_LANES, STATES_PER_TILE + (c + 1) * SCAN_LANES)
            ar = jnp.broadcast_to(are_ref[q, :, re], (nb, SCAN_LANES))
            ai = jnp.broadcast_to(aim_ref[q, :, re], (nb, SCAN_LANES))

            def step(t, carry, re=re, im=im, ar=ar, ai=ai, bu=bu):
                sr, si = carry
                r0 = pl.multiple_of(t * nb, nb)
                nr = ar * sr - ai * si + bu[pl.ds(r0, nb), re]
                ni = ar * si + ai * sr + bu[pl.ds(r0, nb), im]
                bu[pl.ds(r0, nb), re] = nr
                bu[pl.ds(r0, nb), im] = ni
                return nr, ni

            sr, si = lax.fori_loop(0, tc, step, (st_ref[q, :, re], st_ref[q, :, im]), unroll=True)
            st_ref[q, :, re] = sr
            st_ref[q, :, im] = si

        y = jnp.dot(bu[...].astype(BF16), c_ref[q], preferred_element_type=F32)
        y = y + d_ref[:, ch] * u[:, ch]
        z_ref[:, ch] = jax.nn.gelu(y).astype(BF16)

    glu = jnp.dot(z_ref[...], wglu_ref[...], preferred_element_type=F32)
    out = x + glu[:, :d_model] * jax.nn.sigmoid(glu[:, d_model:])
    o_ref[...] = out.reshape(tc, nb, d_model)


def _s5_layer(xt, gain, w_in, b_blk, a_re, a_im, c_blk, d_skip, w_glu, *, tc):
    seq, nb, d = xt.shape
    tiles = d // CH_TILE
    rows = tc * nb
    kern = functools.partial(_s5_kernel, tc=tc, nb=nb, d_model=d)
    return pl.pallas_call(
        kern,
        out_shape=jax.ShapeDtypeStruct(xt.shape, F32),
        grid=(seq // tc,),
        in_specs=[
            pl.BlockSpec((tc, nb, d), lambda i: (i, 0, 0)),
            _const_spec((1, d)),
            _const_spec((d, d)),
            _const_spec((tiles, CH_TILE, 2 * STATES_PER_TILE)),
            _const_spec((tiles, 1, STATES_PER_TILE)),
            _const_spec((tiles, 1, STATES_PER_TILE)),
            _const_spec((tiles, 2 * STATES_PER_TILE, CH_TILE)),
            _const_spec((1, d)),
            _const_spec((d, 2 * d)),
        ],
        out_specs=pl.BlockSpec((tc, nb, d), lambda i: (i, 0, 0)),
        scratch_shapes=[
            pltpu.VMEM((tiles, nb, 2 * STATES_PER_TILE), F32),
            pltpu.VMEM((rows, 2 * STATES_PER_TILE), F32),
            pltpu.VMEM((rows, d), BF16),
        ],
        compiler_params=_params(("arbitrary",)),
        name="s5_layer",
    )(xt, gain.reshape(1, d), w_in.astype(BF16), b_blk,
      a_re.reshape(tiles, 1, STATES_PER_TILE), a_im.reshape(tiles, 1, STATES_PER_TILE),
      c_blk, d_skip.reshape(1, d), w_glu.astype(BF16))


FF_CHUNK = MXU_TILE


def _ffn_body(x, g_ref, wgu_ref, wd_ref, d_ff):
    h = _rms(x, g_ref[...]).astype(BF16)
    acc = jnp.zeros(x.shape, F32)
    for c in range(d_ff // FF_CHUNK):
        gate = jnp.dot(h, wgu_ref[:, c * FF_CHUNK:(c + 1) * FF_CHUNK], preferred_element_type=F32)
        up = jnp.dot(h, wgu_ref[:, d_ff + c * FF_CHUNK:d_ff + (c + 1) * FF_CHUNK],
                     preferred_element_type=F32)
        act = (jax.nn.silu(gate) * up).astype(BF16)
        acc = acc + jnp.dot(act, wd_ref[c * FF_CHUNK:(c + 1) * FF_CHUNK, :],
                            preferred_element_type=F32)
    return x + acc


def _ffn_kernel(x_ref, g_ref, wgu_ref, wd_ref, o_ref, *, d_ff):
    o_ref[...] = _ffn_body(x_ref[...], g_ref, wgu_ref, wd_ref, d_ff)


def _proj_ffn_kernel(x_ref, a_ref, wo_ref, g_ref, wgu_ref, wd_ref, o_ref, *, d_ff):
    lane = lax.broadcasted_iota(jnp.int32, a_ref.shape[1:], 1)
    attn = jnp.concatenate([jnp.where(lane < V_DIM, a_ref[2 * p], a_ref[2 * p + 1])
                            for p in range(a_ref.shape[0] // 2)], axis=-1)
    x = x_ref[...] + jnp.dot(attn, wo_ref[...], preferred_element_type=F32)
    o_ref[...] = _ffn_body(x, g_ref, wgu_ref, wd_ref, d_ff)


def _ffn_layer(x2, gain, w_gate_up, w_down, *, tm, attn=None, w_o=None):
    n, d = x2.shape
    d_ff = w_down.shape[0]
    row_spec = pl.BlockSpec((tm, d), lambda i: (i, 0))
    w_specs = [_const_spec((1, d)), _const_spec((d, 2 * d_ff)), _const_spec((d_ff, d))]
    w_args = (gain.reshape(1, d), w_gate_up.astype(BF16), w_down.astype(BF16))
    if attn is None:
        kern, in_specs, args = _ffn_kernel, [row_spec] + w_specs, (x2,) + w_args
    else:
        _, heads, seq, width = attn.shape
        per_b = seq // tm
        kern = _proj_ffn_kernel
        attn_spec = pl.BlockSpec((None, heads, tm, width), lambda i: (i // per_b, 0, i % per_b, 0))
        in_specs = [row_spec, attn_spec, _const_spec((heads // 2 * width, d))] + w_specs
        args = (x2, attn, w_o.astype(BF16)) + w_args
    return pl.pallas_call(
        functools.partial(kern, d_ff=d_ff),
        out_shape=jax.ShapeDtypeStruct((n, d), F32),
        grid=(n // tm,),
        in_specs=in_specs,
        out_specs=row_spec,
        compiler_params=_params(("parallel",)),
        name="ffn_layer",
    )(*args)


def _rope_tables(pos_ref, freq_ref, scale):
    ang = pos_ref[...].astype(F32) * freq_ref[...]
    lane = lax.broadcasted_iota(jnp.int32, ang.shape, 1)
    cos_t = jnp.cos(ang) * scale
    sin_t = jnp.sin(ang) * scale
    x1 = lane < QK_NOPE + HALF_ROPE
    return cos_t, jnp.where(x1, -sin_t, 0.0), jnp.where(x1, 0.0, sin_t)


def _rotate(tn, cos_t, sin_a, sin_b):
    return (tn * cos_t
            + pltpu.roll(tn, HEAD_TILE - HALF_ROPE, 1) * sin_a
            + pltpu.roll(tn, HALF_ROPE, 1) * sin_b)


def _segment_mean_matrix(tiles=1):
    idx = jnp.arange(HEAD_TILE)
    seg = jnp.where(idx < QK_NOPE, 0, jnp.where(idx < QK_DIM, 1, 2))
    same = (seg[:, None] == seg[None, :]) & (seg[:, None] < 2)
    width = jnp.where(seg == 0, QK_NOPE, QK_ROPE).astype(F32)
    e = jnp.where(same, 1.0 / width[None, :], 0.0)
    return jnp.kron(jnp.eye(tiles, dtype=F32), e).astype(BF16)


def _head_tile_freq(inv_freq):
    z = jnp.zeros((QK_NOPE,), F32)
    pad = jnp.zeros((HEAD_TILE - QK_DIM,), F32)
    return jnp.concatenate([z, inv_freq, inv_freq, pad]).reshape(1, HEAD_TILE)


def _head_tile_gain(nope_gain, rope_gain):
    pad = jnp.zeros((HEAD_TILE - QK_DIM,), F32)
    return jnp.concatenate([nope_gain, rope_gain, pad]).reshape(1, HEAD_TILE)


def _kv_kernel(x_ref, pos_ref, freq_ref, g_ref, wa_ref, an_ref, wk_ref, wv_ref, e_ref, gt_ref,
               gc_ref, k_ref, v_ref):
    h = _rms(x_ref[...], g_ref[...]).astype(BF16)
    ckv = jnp.dot(h, wa_ref[...], preferred_element_type=F32)
    c_kv = _rms(ckv[:, :KV_LORA], an_ref[...]).astype(BF16)
    cos_t, sin_a, sin_b = _rope_tables(pos_ref, freq_ref, 1.0)
    gt = gt_ref[...]
    e = e_ref[...]

    kr = ckv[:, KV_LORA:]
    ms = jnp.dot((kr * kr).astype(BF16), e, preferred_element_type=F32)
    kr = _rotate(kr * lax.rsqrt(ms + EPS) * gt, cos_t, sin_a, sin_b)
    lane = lax.broadcasted_iota(jnp.int32, kr.shape, 1)
    tm = kr.shape[0]

    kr_t = kr.T
    kn_t = lax.dot_general(wk_ref[...], c_kv, (((1,), (1,)), ((), ())),
                           preferred_element_type=F32)
    rowid = lax.broadcasted_iota(jnp.int32, (HEAD_TILE, tm), 0)
    gcol = jnp.broadcast_to(gc_ref[...], (HEAD_TILE, tm))
    for hd in range(N_HEADS):
        t = kn_t[hd * HEAD_TILE:(hd + 1) * HEAD_TILE, :]
        nope = t[:QK_NOPE]
        ms = jnp.sum(nope * nope, axis=0, keepdims=True) * (1.0 / QK_NOPE)
        tn = t * lax.rsqrt(ms + EPS) * gcol
        k_ref[hd] = jnp.where(rowid < QK_NOPE, tn, kr_t).astype(BF16)
    v = jnp.dot(c_kv, wv_ref[...], preferred_element_type=F32)
    for hd in range(N_HEADS):
        tile = v[:, hd * HEAD_TILE:(hd + 1) * HEAD_TILE]
        v_ref[hd] = jnp.where(lane == _ones_lane(hd % 2), 1.0, tile).astype(BF16)


def _ones_lane(parity):
    return V_DIM if parity == 0 else HEAD_TILE - V_DIM - 1


def _head_major_spec(heads, tm, seq):
    per_b = seq // tm
    return pl.BlockSpec((None, heads, tm, HEAD_TILE), lambda i: (i // per_b, 0, i % per_b, 0))


def _kv_proj(x2, pos2, inv_freq, gain, w_kv_a, kv_a_norm, w_kv_b, k_nope_norm, k_rope_norm, *,
             tm, bsz):
    n, d = x2.shape
    seq = n // bsz
    wa_rope = jnp.zeros((d, HEAD_TILE), F32).at[:, QK_NOPE:QK_DIM].set(w_kv_a[:, KV_LORA:])
    wa = jnp.concatenate([w_kv_a[:, :KV_LORA], wa_rope], axis=1).astype(BF16)
    wb = w_kv_b.reshape(KV_LORA, N_HEADS, QK_NOPE + V_DIM)
    wk = jnp.zeros((KV_LORA, N_HEADS, HEAD_TILE), F32).at[:, :, :QK_NOPE].set(wb[:, :, :QK_NOPE])
    wk = wk.reshape(KV_LORA, N_HEADS * HEAD_TILE).T.astype(BF16)
    wv = jnp.zeros((KV_LORA, N_HEADS // 2, 2, HEAD_TILE), F32)
    wv = wv.at[:, :, 0, :V_DIM].set(wb[:, 0::2, QK_NOPE:]).at[:, :, 1, V_DIM:].set(wb[:, 1::2, QK_NOPE:])
    wv = wv.reshape(KV_LORA, N_HEADS * HEAD_TILE).astype(BF16)
    gain_tile = _head_tile_gain(k_nope_norm, k_rope_norm)
    row = lambda w: pl.BlockSpec((tm, w), lambda i: (i, 0))
    per_b = seq // tm
    kt_spec = pl.BlockSpec((None, N_HEADS, HEAD_TILE, tm), lambda i: (i // per_b, 0, 0, i % per_b))
    return pl.pallas_call(
        _kv_kernel,
        out_shape=[jax.ShapeDtypeStruct((bsz, N_HEADS, HEAD_TILE, seq), BF16),
                   jax.ShapeDtypeStruct((bsz, N_HEADS, seq, HEAD_TILE), BF16)],
        grid=(n // tm,),
        in_specs=[row(d), row(1), _const_spec((1, HEAD_TILE)), _const_spec((1, d)),
                  _const_spec(wa.shape), _const_spec((1, KV_LORA)), _const_spec(wk.shape),
                  _const_spec(wv.shape), _const_spec((HEAD_TILE, HEAD_TILE)),
                  _const_spec((1, HEAD_TILE)), _const_spec((HEAD_TILE, 1))],
        out_specs=[kt_spec, _head_major_spec(N_HEADS, tm, seq)],
        compiler_params=_params(("parallel",)),
        name="kv_proj",
    )(x2, pos2, _head_tile_freq(inv_freq), gain.reshape(1, d), wa, kv_a_norm.reshape(1, KV_LORA),
      wk, wv, _segment_mean_matrix(), gain_tile, gain_tile.reshape(HEAD_TILE, 1))


def _q_kernel(x_ref, pos_ref, freq_ref, g_ref, wa_ref, an_ref, wb_ref, e_ref, gt_ref, q_ref, *, scale):
    h = _rms(x_ref[...], g_ref[...]).astype(BF16)
    c_q = _rms(jnp.dot(h, wa_ref[...], preferred_element_type=F32), an_ref[...]).astype(BF16)
    q = jnp.dot(c_q, wb_ref[...], preferred_element_type=F32)
    cos_t, sin_a, sin_b = _rope_tables(pos_ref, freq_ref, scale)
    gt = gt_ref[...]
    e = e_ref[...]
    for pair in range(N_HEADS // 2):
        t2 = q[:, 2 * pair * HEAD_TILE:(2 * pair + 2) * HEAD_TILE]
        ms2 = jnp.dot((t2 * t2).astype(BF16), e, preferred_element_type=F32)
        for k in range(2):
            sl = slice(k * HEAD_TILE, (k + 1) * HEAD_TILE)
            tn = t2[:, sl] * lax.rsqrt(ms2[:, sl] + EPS) * gt
            q_ref[2 * pair + k] = _rotate(tn, cos_t, sin_a, sin_b).astype(BF16)


def _q_proj(x2, pos2, inv_freq, gain, w_q_a, q_a_norm, w_q_b, q_nope_norm, q_rope_norm, *, tm, bsz):
    n, d = x2.shape
    seq = n // bsz
    rank = w_q_a.shape[1]
    wb = w_q_b.reshape(rank, N_HEADS, QK_DIM)
    wb = jnp.zeros((rank, N_HEADS, HEAD_TILE), F32).at[:, :, :QK_DIM].set(wb)
    wb = wb.reshape(rank, N_HEADS * HEAD_TILE).astype(BF16)
    row = lambda w: pl.BlockSpec((tm, w), lambda i: (i, 0))
    return pl.pallas_call(
        functools.partial(_q_kernel, scale=math.log2(math.e) / math.sqrt(QK_DIM)),
        out_shape=jax.ShapeDtypeStruct((bsz, N_HEADS, seq, HEAD_TILE), BF16),
        grid=(n // tm,),
        in_specs=[row(d), row(1), _const_spec((1, HEAD_TILE)), _const_spec((1, d)),
                  _const_spec((d, rank)), _const_spec((1, rank)), _const_spec(wb.shape),
                  _const_spec((2 * HEAD_TILE, 2 * HEAD_TILE)), _const_spec((1, HEAD_TILE))],
        out_specs=_head_major_spec(N_HEADS, tm, seq),
        compiler_params=_params(("parallel",)),
        name="q_proj",
    )(x2, pos2, _head_tile_freq(inv_freq), gain.reshape(1, d), w_q_a.astype(BF16),
      q_a_norm.reshape(1, rank), wb, _segment_mean_matrix(2),
      _head_tile_gain(q_nope_norm, q_rope_norm))


MAX_HEADS_IN_FLIGHT = 8
SCORE_SCRATCH_SEQS = 4


def _attn_kernel(q_ref, kt_ref, v_ref, o_ref, s_ref, m_ref, p_ref, *, tq, nq):
    i = pl.program_id(1)
    nh = q_ref.shape[0]
    row = lax.broadcasted_iota(jnp.int32, (tq, tq), 0)
    col = lax.broadcasted_iota(jnp.int32, (tq, tq), 1)
    causal = col <= row
    lane_tiles = tq // LANES

    def stage_a(h, base, ms, nk):
        qh = q_ref[h]
        mx = None
        for j in range(nk):
            s = jnp.dot(qh, kt_ref[h, :, j * tq:(j + 1) * tq], preferred_element_type=F32)
            if j == nk - 1:
                s = jnp.where(causal, s, NEG_INF)
            s_ref[:, base + j * tq:base + (j + 1) * tq] = s
            for t in range(lane_tiles):
                f = s[:, t * LANES:(t + 1) * LANES]
                mx = f if mx is None else jnp.maximum(mx, f)
        m = jnp.max(mx, axis=-1, keepdims=True)
        m_ref[ms] = jnp.broadcast_to(m, (tq, LANES))

    def stage_b(h, base, ms, nk):
        m = m_ref[ms]
        m = jnp.concatenate([m] * lane_tiles, axis=-1)
        for j in range(nk):
            cols = slice(base + j * tq, base + (j + 1) * tq)
            p_ref[:, cols] = jnp.exp2((s_ref[:, cols] - m).astype(BF16))
        acc = jnp.dot(p_ref[:, base:base + nk * tq], v_ref[h, :nk * tq, :],
                      preferred_element_type=F32)
        odd = (h % 2) == 1
        denom = jnp.where(odd, acc[:, _ones_lane(1):_ones_lane(1) + 1],
                          acc[:, _ones_lane(0):_ones_lane(0) + 1])
        o_ref[h] = (acc / denom).astype(BF16)

    for c in range(nq):
        @pl.when(i == c)
        def _(c=c):
            nk = c + 1
            hpi = min(m_ref.shape[0], s_ref.shape[1] // (nk * tq))
            hpi = 1 << (hpi.bit_length() - 1)

            def body(it, carry):
                for k in range(hpi):
                    stage_a(it * hpi + k, k * nk * tq, k, nk)
                for k in range(hpi):
                    stage_b(it * hpi + k, k * nk * tq, k, nk)
                return carry
            lax.fori_loop(0, nh // hpi, body, 0)


def _attention(q, kt, v, *, tq):
    bsz, nh, seq, _ = q.shape
    nq = seq // tq
    head_tile = lambda rows: pl.BlockSpec((None, nh, rows, HEAD_TILE), lambda b, i: (b, 0, i, 0))
    whole = lambda shape: pl.BlockSpec((None,) + shape, lambda b, i: (b, 0, 0, 0))
    return pl.pallas_call(
        functools.partial(_attn_kernel, tq=tq, nq=nq),
        out_shape=jax.ShapeDtypeStruct((bsz, nh, seq, HEAD_TILE), BF16),
        grid=(bsz, nq),
        in_specs=[head_tile(tq), whole((nh, HEAD_TILE, seq)), whole((nh, seq, HEAD_TILE))],
        out_specs=head_tile(tq),
        scratch_shapes=[pltpu.VMEM((tq, SCORE_SCRATCH_SEQS * seq), F32),
                        pltpu.VMEM((MAX_HEADS_IN_FLIGHT, tq, LANES), F32),
                        pltpu.VMEM((tq, SCORE_SCRATCH_SEQS * seq), BF16)],
        compiler_params=_params(("parallel", "arbitrary")),
        name="causal_attention",
    )(q, kt, v)


def _tiles(bsz, seq):
    tm = 512 if seq % 512 == 0 else seq
    tc = 16 if seq % 16 == 0 else seq
    tq = 256 if seq % 256 == 0 else seq
    return tm, tc, tq


def kernel(x, positions, mix_norm, ffn_norm, ffn_w_gate_up, ffn_w_down, ssm_w_in, ssm_lambda_re, ssm_lambda_im, ssm_log_step, ssm_b_re, ssm_b_im, ssm_c_re, ssm_c_im, ssm_d, ssm_w_glu, kv_in_norm, mla_w_kv_a, mla_kv_a_norm, mla_w_kv_b, mla_k_nope_norm, mla_k_rope_norm, mla_w_q_a, mla_q_a_norm, mla_w_q_b, mla_q_nope_norm, mla_q_rope_norm, mla_w_o):
    bsz, seq, d = x.shape
    n = bsz * seq
    depth = mix_norm.shape[0]
    n_ssm = ssm_w_in.shape[0]
    assert bsz % 8 == 0 and d % CH_TILE == 0
    tm, tc, tq = _tiles(bsz, seq)

    xt = jnp.transpose(x, (1, 0, 2))
    for i in range(n_ssm):
        a_re, a_im, br, bi = _discretise(ssm_lambda_re[i], ssm_lambda_im[i], ssm_log_step[i],
                                         ssm_b_re[i], ssm_b_im[i])
        b_blk, c_blk = _block_diag_weights(br, bi, ssm_c_re[i], ssm_c_im[i])
        xt = _s5_layer(xt, mix_norm[i], ssm_w_in[i], b_blk, a_re, a_im, c_blk, ssm_d[i],
                       ssm_w_glu[i], tc=tc)
        xt = _ffn_layer(xt.reshape(n, d), ffn_norm[i], ffn_w_gate_up[i], ffn_w_down[i],
                        tm=tm).reshape(seq, bsz, d)
    x2 = jnp.transpose(xt, (1, 0, 2)).reshape(n, d)

    pos2 = positions.reshape(n, 1)
    inv_freq = ROPE_THETA ** (-jnp.arange(0, QK_ROPE, 2, dtype=F32) / QK_ROPE)
    k, v = _kv_proj(x2, pos2, inv_freq, kv_in_norm, mla_w_kv_a, mla_kv_a_norm, mla_w_kv_b,
                    mla_k_nope_norm, mla_k_rope_norm, tm=tm, bsz=bsz)
    for layer in range(n_ssm, depth):
        j = layer - n_ssm
        q = _q_proj(x2, pos2, inv_freq, mix_norm[layer], mla_w_q_a[j], mla_q_a_norm[j],
                    mla_w_q_b[j], mla_q_nope_norm[j], mla_q_rope_norm[j], tm=tm, bsz=bsz)
        o = _attention(q, k, v, tq=tq)
        x2 = _ffn_layer(x2, ffn_norm[layer], ffn_w_gate_up[layer], ffn_w_down[layer], tm=tm,
                        attn=o, w_o=mla_w_o[j])
    return x2.reshape(bsz, seq, d)
```

```python
import functools
import math

import jax
import jax.numpy as jnp
from jax import lax
from jax.experimental import pallas as pl
from jax.experimental.pallas import tpu as pltpu

F32 = jnp.float32
BF16 = jnp.bfloat16

EPS = 1e-6
NEG_INF = -1e30
ROPE_THETA = 10000.0

SSM_GROUP = 16
SSM_STATE = 64
N_HEADS = 16
QK_NOPE = 64
QK_ROPE = 32
HALF_ROPE = QK_ROPE // 2
QK_DIM = QK_NOPE + QK_ROPE
V_DIM = 64
KV_LORA = 256

LANES = 128
MXU_TILE = 256
VMEM_LIMIT = 56 * 1024 * 1024

HEAD_TILE = LANES
CH_TILE = MXU_TILE
GROUPS_PER_TILE = CH_TILE // SSM_GROUP
STATES_PER_TILE = GROUPS_PER_TILE * SSM_STATE


def _rms(x, gain):
    ms = jnp.mean(x * x, axis=-1, keepdims=True)
    return x * lax.rsqrt(ms + EPS) * gain


def _const_spec(shape):
    nd = len(shape)
    return pl.BlockSpec(shape, lambda *_: (0,) * nd, pipeline_mode=pl.Buffered(1))


def _params(semantics):
    return pltpu.CompilerParams(dimension_semantics=semantics, vmem_limit_bytes=VMEM_LIMIT)


def _disc_kernel(lr_ref, li_ref, ls_ref, bre_ref, bim_ref, are_ref, aim_ref, br_ref, bi_ref):
    lr = lr_ref[...]
    li = li_ref[...]
    step = jnp.exp(ls_ref[...])
    decay = jnp.exp(lr * step)
    a_re = decay * jnp.cos(li * step)
    a_im = decay * jnp.sin(li * step)
    den = lr * lr + li * li
    f_re = ((a_re - 1.0) * lr + a_im * li) / den
    f_im = (a_im * lr - (a_re - 1.0) * li) / den
    b_re = bre_ref[...]
    b_im = bim_ref[...]
    are_ref[...] = a_re
    aim_ref[...] = a_im
    br_ref[...] = f_re * b_re - f_im * b_im
    bi_ref[...] = f_re * b_im + f_im * b_re


def _discretise(lam_re, lam_im, log_step, b_re, b_im):
    g, n = lam_re.shape
    k = b_re.shape[-1]
    gn = g * n
    col = lambda a: a.reshape(gn, 1)
    ls = jnp.broadcast_to(log_step[:, None], (g, n))
    outs = pl.pallas_call(
        _disc_kernel,
        out_shape=[jax.ShapeDtypeStruct((gn, 1), F32), jax.ShapeDtypeStruct((gn, 1), F32),
                   jax.ShapeDtypeStruct((gn, k), F32), jax.ShapeDtypeStruct((gn, k), F32)],
        name="s5_discretise",
    )(col(lam_re), col(lam_im), col(ls), b_re.reshape(gn, k), b_im.reshape(gn, k))
    a_re, a_im, br, bi = outs
    return a_re.reshape(g, n), a_im.reshape(g, n), br.reshape(g, n, k), bi.reshape(g, n, k)


def _block_diag_weights(br, bi, c_re, c_im):
    g, n, k = br.shape
    tiles = g // GROUPS_PER_TILE
    eye = jnp.eye(GROUPS_PER_TILE, dtype=F32)

    def bmat(b):
        b4 = b.reshape(tiles, GROUPS_PER_TILE, n, k)
        return jnp.einsum('qgnk,gh->qgkhn', b4, eye).reshape(tiles, CH_TILE, STATES_PER_TILE)

    def cmat(c):
        c4 = c.reshape(tiles, GROUPS_PER_TILE, k, n)
        return jnp.einsum('qgkn,gh->qgnhk', c4, eye).reshape(tiles, STATES_PER_TILE, CH_TILE)

    b_blk = jnp.concatenate([bmat(br), bmat(bi)], axis=-1).astype(BF16)
    c_blk = jnp.concatenate([cmat(c_re), -cmat(c_im)], axis=1).astype(BF16)
    return b_blk, c_blk


SCAN_LANES = 256


def _s5_kernel(x_ref, g_ref, win_ref, b_ref, are_ref, aim_ref, c_ref, d_ref, wglu_ref,
               o_ref, st_ref, bu_ref, z_ref, *, tc, nb, d_model, batch_major_in):
    rows = tc * nb
    tiles = d_model // CH_TILE

    @pl.when(pl.program_id(0) == 0)
    def _():
        st_ref[...] = jnp.zeros_like(st_ref)

    x = x_ref[...]
    if batch_major_in:
        x = pltpu.einshape("btd->tbd", x)
    x = x.reshape(rows, d_model)
    h = _rms(x, g_ref[...]).astype(BF16)
    u = jnp.dot(h, win_ref[...], preferred_element_type=F32)
    u_bf = u.astype(BF16)

    for q in range(tiles):
        ch = slice(q * CH_TILE, (q + 1) * CH_TILE)
        bu = bu_ref
        bu[...] = jnp.dot(u_bf[:, ch], b_ref[q], preferred_element_type=F32)

        for c in range(STATES_PER_TILE // SCAN_LANES):
            re = slice(c * SCAN_LANES, (c + 1) * SCAN_LANES)
            im = slice(STATES_PER_TILE + c * SCAN_LANES, STATES_PER_TILE + (c + 1) * SCAN_LANES)
            ar = jnp.broadcast_to(are_ref[q, :, re], (nb, SCAN_LANES))
            ai = jnp.broadcast_to(aim_ref[q, :, re], (nb, SCAN_LANES))

            def step(t, carry, re=re, im=im, ar=ar, ai=ai, bu=bu):
                sr, si = carry
                r0 = pl.multiple_of(t * nb, nb)
                nr = ar * sr - ai * si + bu[pl.ds(r0, nb), re]
                ni = ar * si + ai * sr + bu[pl.ds(r0, nb), im]
                bu[pl.ds(r0, nb), re] = nr
                bu[pl.ds(r0, nb), im] = ni
                return nr, ni

            sr, si = lax.fori_loop(0, tc, step, (st_ref[q, :, re], st_ref[q, :, im]), unroll=True)
            st_ref[q, :, re] = sr
            st_ref[q, :, im] = si

        y = jnp.dot(bu[...].astype(BF16), c_ref[q], preferred_element_type=F32)
        y = y + d_ref[:, ch] * u[:, ch]
        z_ref[:, ch] = jax.nn.gelu(y).astype(BF16)

    glu = jnp.dot(z_ref[...], wglu_ref[...], preferred_element_type=F32)
    out = x + glu[:, :d_model] * jax.nn.sigmoid(glu[:, d_model:])
    o_ref[...] = out.reshape(tc, nb, d_model)


def _s5_layer(xin, gain, w_in, b_blk, a_re, a_im, c_blk, d_skip, w_glu, *, tc, batch_major_in):
    if batch_major_in:
        nb, seq, d = xin.shape
        x_spec = pl.BlockSpec((nb, tc, d), lambda i: (0, i, 0))
    else:
        seq, nb, d = xin.shape
        x_spec = pl.BlockSpec((tc, nb, d), lambda i: (i, 0, 0))
    tiles = d // CH_TILE
    rows = tc * nb
    kern = functools.partial(_s5_kernel, tc=tc, nb=nb, d_model=d, batch_major_in=batch_major_in)
    return pl.pallas_call(
        kern,
        out_shape=jax.ShapeDtypeStruct((seq, nb, d), F32),
        grid=(seq // tc,),
        in_specs=[
            x_spec,
            _const_spec((1, d)),
            _const_spec((d, d)),
            _const_spec((tiles, CH_TILE, 2 * STATES_PER_TILE)),
            _const_spec((tiles, 1, STATES_PER_TILE)),
            _const_spec((tiles, 1, STATES_PER_TILE)),
            _const_spec((tiles, 2 * STATES_PER_TILE, CH_TILE)),
            _const_spec((1, d)),
            _const_spec((d, 2 * d)),
        ],
        out_specs=pl.BlockSpec((tc, nb, d), lambda i: (i, 0, 0)),
        scratch_shapes=[
            pltpu.VMEM((tiles, nb, 2 * STATES_PER_TILE), F32),
            pltpu.VMEM((rows, 2 * STATES_PER_TILE), F32),
            pltpu.VMEM((rows, d), BF16),
        ],
        compiler_params=_params(("arbitrary",)),
        name="s5_layer",
    )(xin, gain.reshape(1, d), w_in.astype(BF16), b_blk,
      a_re.reshape(tiles, 1, STATES_PER_TILE), a_im.reshape(tiles, 1, STATES_PER_TILE),
      c_blk, d_skip.reshape(1, d), w_glu.astype(BF16))


FF_CHUNK = MXU_TILE


def _ffn_body(x, g_ref, wgu_ref, wd_ref, d_ff):
    h = _rms(x, g_ref[...]).astype(BF16)
    acc = jnp.zeros(x.shape, F32)
    for c in range(d_ff // FF_CHUNK):
        gate = jnp.dot(h, wgu_ref[:, c * FF_CHUNK:(c + 1) * FF_CHUNK], preferred_element_type=F32)
        up = jnp.dot(h, wgu_ref[:, d_ff + c * FF_CHUNK:d_ff + (c + 1) * FF_CHUNK],
                     preferred_element_type=F32)
        act = (jax.nn.silu(gate) * up).astype(BF16)
        acc = acc + jnp.dot(act, wd_ref[c * FF_CHUNK:(c + 1) * FF_CHUNK, :],
                            preferred_element_type=F32)
    return x + acc


def _ffn_kernel(x_ref, g_ref, wgu_ref, wd_ref, o_ref, *, d_ff):
    out = _ffn_body(x_ref[...], g_ref, wgu_ref, wd_ref, d_ff)
    if len(o_ref.shape) == 3:
        nb, steps, d = o_ref.shape
        out = pltpu.einshape("tbd->btd", out.reshape(steps, nb, d))
    o_ref[...] = out


def _proj_ffn_kernel(x_ref, a_ref, wo_ref, g_ref, wgu_ref, wd_ref, o_ref, *, d_ff):
    lane = lax.broadcasted_iota(jnp.int32, a_ref.shape[1:], 1)
    attn = jnp.concatenate([jnp.where(lane < V_DIM, a_ref[2 * p], a_ref[2 * p + 1])
                            for p in range(a_ref.shape[0] // 2)], axis=-1)
    x = x_ref[...] + jnp.dot(attn, wo_ref[...], preferred_element_type=F32)
    o_ref[...] = _ffn_body(x, g_ref, wgu_ref, wd_ref, d_ff)


def _ffn_layer(x2, gain, w_gate_up, w_down, *, tm, attn=None, w_o=None, batch_major_out=None):
    n, d = x2.shape
    d_ff = w_down.shape[0]
    row_spec = pl.BlockSpec((tm, d), lambda i: (i, 0))
    out_shape, out_spec = (n, d), row_spec
    if batch_major_out is not None:
        nb = batch_major_out
        out_shape = (nb, n // nb, d)
        out_spec = pl.BlockSpec((nb, tm // nb, d), lambda i: (0, i, 0))
    w_specs = [_const_spec((1, d)), _const_spec((d, 2 * d_ff)), _const_spec((d_ff, d))]
    w_args = (gain.reshape(1, d), w_gate_up.astype(BF16), w_down.astype(BF16))
    if attn is None:
        kern, in_specs, args = _ffn_kernel, [row_spec] + w_specs, (x2,) + w_args
    else:
        _, heads, seq, width = attn.shape
        per_b = seq // tm
        kern = _proj_ffn_kernel
        attn_spec = pl.BlockSpec((None, heads, tm, width), lambda i: (i // per_b, 0, i % per_b, 0))
        in_specs = [row_spec, attn_spec, _const_spec((heads // 2 * width, d))] + w_specs
        args = (x2, attn, w_o.astype(BF16)) + w_args
    return pl.pallas_call(
        functools.partial(kern, d_ff=d_ff),
        out_shape=jax.ShapeDtypeStruct(out_shape, F32),
        grid=(n // tm,),
        in_specs=in_specs,
        out_specs=out_spec,
        compiler_params=_params(("parallel",)),
        name="ffn_layer",
    )(*args)


def _rope_tables(pos_ref, freq_ref):
    ang = pos_ref[...].astype(F32) * freq_ref[...]
    return jnp.cos(ang), jnp.sin(ang)


def _rotate(tn, cos_t, sin_t, swap):
    partner = jnp.dot(tn.astype(BF16), swap, preferred_element_type=F32)
    return tn * cos_t + partner * sin_t


def _rope_swap_matrix(tiles=1):
    i = jnp.arange(HALF_ROPE)
    p = jnp.zeros((HEAD_TILE, HEAD_TILE), F32)
    p = p.at[QK_NOPE + HALF_ROPE + i, QK_NOPE + i].set(-1.0)
    p = p.at[QK_NOPE + i, QK_NOPE + HALF_ROPE + i].set(1.0)
    return jnp.kron(jnp.eye(tiles, dtype=F32), p).astype(BF16)


def _segment_mean_matrix(tiles=1):
    idx = jnp.arange(HEAD_TILE)
    seg = jnp.where(idx < QK_NOPE, 0, jnp.where(idx < QK_DIM, 1, 2))
    same = (seg[:, None] == seg[None, :]) & (seg[:, None] < 2)
    width = jnp.where(seg == 0, QK_NOPE, QK_ROPE).astype(F32)
    e = jnp.where(same, 1.0 / width[None, :], 0.0)
    return jnp.kron(jnp.eye(tiles, dtype=F32), e).astype(BF16)


def _head_tile_freq(inv_freq):
    z = jnp.zeros((QK_NOPE,), F32)
    pad = jnp.zeros((HEAD_TILE - QK_DIM,), F32)
    return jnp.concatenate([z, inv_freq, inv_freq, pad]).reshape(1, HEAD_TILE)


def _head_tile_gain(nope_gain, rope_gain):
    pad = jnp.zeros((HEAD_TILE - QK_DIM,), F32)
    return jnp.concatenate([nope_gain, rope_gain, pad]).reshape(1, HEAD_TILE)


def _kv_kernel(x_ref, pos_ref, freq_ref, g_ref, wa_ref, an_ref, wk_ref, wv_ref, e_ref, sw_ref,
               gt_ref, gc_ref, k_ref, v_ref, cos_ref, sin_ref):
    h = _rms(x_ref[...], g_ref[...]).astype(BF16)
    ckv = jnp.dot(h, wa_ref[...], preferred_element_type=F32)
    c_kv = _rms(ckv[:, :KV_LORA], an_ref[...]).astype(BF16)
    cos_t, sin_t = _rope_tables(pos_ref, freq_ref)
    cos_ref[...] = cos_t
    sin_ref[...] = sin_t
    gt = gt_ref[...]
    e = e_ref[...]

    kr = ckv[:, KV_LORA:]
    ms = jnp.dot((kr * kr).astype(BF16), e, preferred_element_type=F32)
    kr = _rotate(kr * lax.rsqrt(ms + EPS) * gt, cos_t, sin_t, sw_ref[...])
    lane = lax.broadcasted_iota(jnp.int32, kr.shape, 1)
    tm = kr.shape[0]

    kr_t = kr.T
    kn_t = lax.dot_general(wk_ref[...], c_kv, (((1,), (1,)), ((), ())),
                           preferred_element_type=F32)
    rowid = lax.broadcasted_iota(jnp.int32, (HEAD_TILE, tm), 0)
    gcol = jnp.broadcast_to(gc_ref[...], (HEAD_TILE, tm))
    for hd in range(N_HEADS):
        t = kn_t[hd * HEAD_TILE:(hd + 1) * HEAD_TILE, :]
        nope = t[:QK_NOPE]
        ms = jnp.sum(nope * nope, axis=0, keepdims=True) * (1.0 / QK_NOPE)
        tn = t * lax.rsqrt(ms + EPS) * gcol
        k_ref[hd] = jnp.where(rowid < QK_NOPE, tn, kr_t).astype(BF16)
    v = jnp.dot(c_kv, wv_ref[...], preferred_element_type=F32)
    for hd in range(N_HEADS):
        tile = v[:, hd * HEAD_TILE:(hd + 1) * HEAD_TILE]
        v_ref[hd] = jnp.where(lane == _ones_lane(hd % 2), 1.0, tile).astype(BF16)


def _ones_lane(parity):
    return V_DIM if parity == 0 else HEAD_TILE - V_DIM - 1


def _head_major_spec(heads, tm, seq):
    per_b = seq // tm
    return pl.BlockSpec((None, heads, tm, HEAD_TILE), lambda i: (i // per_b, 0, i % per_b, 0))


def _kv_proj(x2, pos2, inv_freq, gain, w_kv_a, kv_a_norm, w_kv_b, k_nope_norm, k_rope_norm, *,
             tm, bsz):
    n, d = x2.shape
    seq = n // bsz
    wa_rope = jnp.zeros((d, HEAD_TILE), F32).at[:, QK_NOPE:QK_DIM].set(w_kv_a[:, KV_LORA:])
    wa = jnp.concatenate([w_kv_a[:, :KV_LORA], wa_rope], axis=1).astype(BF16)
    wb = w_kv_b.reshape(KV_LORA, N_HEADS, QK_NOPE + V_DIM)
    wk = jnp.zeros((KV_LORA, N_HEADS, HEAD_TILE), F32).at[:, :, :QK_NOPE].set(wb[:, :, :QK_NOPE])
    wk = wk.reshape(KV_LORA, N_HEADS * HEAD_TILE).T.astype(BF16)
    wv = jnp.zeros((KV_LORA, N_HEADS // 2, 2, HEAD_TILE), F32)
    wv = wv.at[:, :, 0, :V_DIM].set(wb[:, 0::2, QK_NOPE:]).at[:, :, 1, V_DIM:].set(wb[:, 1::2, QK_NOPE:])
    wv = wv.reshape(KV_LORA, N_HEADS * HEAD_TILE).astype(BF16)
    gain_tile = _head_tile_gain(k_nope_norm, k_rope_norm)
    row = lambda w: pl.BlockSpec((tm, w), lambda i: (i, 0))
    per_b = seq // tm
    kt_spec = pl.BlockSpec((None, N_HEADS, HEAD_TILE, tm), lambda i: (i // per_b, 0, 0, i % per_b))
    return pl.pallas_call(
        _kv_kernel,
        out_shape=[jax.ShapeDtypeStruct((bsz, N_HEADS, HEAD_TILE, seq), BF16),
                   jax.ShapeDtypeStruct((bsz, N_HEADS, seq, HEAD_TILE), BF16),
                   jax.ShapeDtypeStruct((n, HEAD_TILE), F32),
                   jax.ShapeDtypeStruct((n, HEAD_TILE), F32)],
        grid=(n // tm,),
        in_specs=[row(d), row(1), _const_spec((1, HEAD_TILE)), _const_spec((1, d)),
                  _const_spec(wa.shape), _const_spec((1, KV_LORA)), _const_spec(wk.shape),
                  _const_spec(wv.shape), _const_spec((HEAD_TILE, HEAD_TILE)),
                  _const_spec((HEAD_TILE, HEAD_TILE)),
                  _const_spec((1, HEAD_TILE)), _const_spec((HEAD_TILE, 1))],
        out_specs=[kt_spec, _head_major_spec(N_HEADS, tm, seq), row(HEAD_TILE), row(HEAD_TILE)],
        compiler_params=_params(("parallel",)),
        name="kv_proj",
    )(x2, pos2, _head_tile_freq(inv_freq), gain.reshape(1, d), wa, kv_a_norm.reshape(1, KV_LORA),
      wk, wv, _segment_mean_matrix(), _rope_swap_matrix(), gain_tile,
      gain_tile.reshape(HEAD_TILE, 1))


def _q_kernel(x_ref, cos_ref, sin_ref, g_ref, wa_ref, an_ref, wb_ref, e_ref, sw_ref, gt_ref, q_ref, *,
              scale):
    h = _rms(x_ref[...], g_ref[...]).astype(BF16)
    c_q = _rms(jnp.dot(h, wa_ref[...], preferred_element_type=F32), an_ref[...]).astype(BF16)
    q = jnp.dot(c_q, wb_ref[...], preferred_element_type=F32)
    cos2 = jnp.concatenate([cos_ref[...] * scale] * 2, axis=-1)
    sin2 = jnp.concatenate([sin_ref[...] * scale] * 2, axis=-1)
    gt2 = jnp.concatenate([gt_ref[...]] * 2, axis=-1)
    e = e_ref[...]
    swap = sw_ref[...]
    for pair in range(N_HEADS // 2):
        t2 = q[:, 2 * pair * HEAD_TILE:(2 * pair + 2) * HEAD_TILE]
        ms2 = jnp.dot((t2 * t2).astype(BF16), e, preferred_element_type=F32)
        out = _rotate(t2 * lax.rsqrt(ms2 + EPS) * gt2, cos2, sin2, swap).astype(BF16)
        q_ref[2 * pair] = out[:, :HEAD_TILE]
        q_ref[2 * pair + 1] = out[:, HEAD_TILE:]


def _q_proj(x2, cos_t, sin_t, gain, w_q_a, q_a_norm, w_q_b, q_nope_norm, q_rope_norm, *, tm, bsz):
    n, d = x2.shape
    seq = n // bsz
    rank = w_q_a.shape[1]
    wb = w_q_b.reshape(rank, N_HEADS, QK_DIM)
    wb = jnp.zeros((rank, N_HEADS, HEAD_TILE), F32).at[:, :, :QK_DIM].set(wb)
    wb = wb.reshape(rank, N_HEADS * HEAD_TILE).astype(BF16)
    row = lambda w: pl.BlockSpec((tm, w), lambda i: (i, 0))
    return pl.pallas_call(
        functools.partial(_q_kernel, scale=math.log2(math.e) / math.sqrt(QK_DIM)),
        out_shape=jax.ShapeDtypeStruct((bsz, N_HEADS, seq, HEAD_TILE), BF16),
        grid=(n // tm,),
        in_specs=[row(d), row(HEAD_TILE), row(HEAD_TILE), _const_spec((1, d)),
                  _const_spec((d, rank)), _const_spec((1, rank)), _const_spec(wb.shape),
                  _const_spec((2 * HEAD_TILE, 2 * HEAD_TILE)),
                  _const_spec((2 * HEAD_TILE, 2 * HEAD_TILE)), _const_spec((1, HEAD_TILE))],
        out_specs=_head_major_spec(N_HEADS, tm, seq),
        compiler_params=_params(("parallel",)),
        name="q_proj",
    )(x2, cos_t, sin_t, gain.reshape(1, d), w_q_a.astype(BF16),
      q_a_norm.reshape(1, rank), wb, _segment_mean_matrix(2), _rope_swap_matrix(2),
      _head_tile_gain(q_nope_norm, q_rope_norm))


MAX_HEADS_IN_FLIGHT = 8
SCORE_SCRATCH_SEQS = 4


def _attn_kernel(q_ref, kt_ref, v_ref, o_ref, s_ref, m_ref, p_ref, *, tq, nq):
    i = pl.program_id(1)
    nh = q_ref.shape[0]
    row = lax.broadcasted_iota(jnp.int32, (tq, tq), 0)
    col = lax.broadcasted_iota(jnp.int32, (tq, tq), 1)
    causal = col <= row
    lane_tiles = tq // LANES

    def stage_a(h, base, ms, nk):
        qh = q_ref[h]
        mx = None
        for j in range(nk):
            s = jnp.dot(qh, kt_ref[h, :, j * tq:(j + 1) * tq], preferred_element_type=F32)
            if j == nk - 1:
                s = jnp.where(causal, s, NEG_INF)
            s_ref[:, base + j * tq:base + (j + 1) * tq] = s
            for t in range(lane_tiles):
                f = s[:, t * LANES:(t + 1) * LANES]
                mx = f if mx is None else jnp.maximum(mx, f)
        m = jnp.max(mx, axis=-1, keepdims=True)
        m_ref[ms] = jnp.broadcast_to(m, (tq, LANES))

    def stage_b(h, base, ms, nk):
        m = m_ref[ms]
        m = jnp.concatenate([m] * lane_tiles, axis=-1)
        for j in range(nk):
            cols = slice(base + j * tq, base + (j + 1) * tq)
            p_ref[:, cols] = jnp.exp2((s_ref[:, cols] - m).astype(BF16))
        acc = jnp.dot(p_ref[:, base:base + nk * tq], v_ref[h, :nk * tq, :],
                      preferred_element_type=F32)
        odd = (h % 2) == 1
        denom = jnp.where(odd, acc[:, _ones_lane(1):_ones_lane(1) + 1],
                          acc[:, _ones_lane(0):_ones_lane(0) + 1])
        o_ref[h] = (acc / denom).astype(BF16)

    for c in range(nq):
        @pl.when(i == c)
        def _(c=c):
            nk = c + 1
            hpi = min(m_ref.shape[0], s_ref.shape[1] // (nk * tq))
            hpi = 1 << (hpi.bit_length() - 1)

            def body(it, carry):
                for k in range(hpi):
                    stage_a(it * hpi + k, k * nk * tq, k, nk)
                for k in range(hpi):
                    stage_b(it * hpi + k, k * nk * tq, k, nk)
                return carry
            lax.fori_loop(0, nh // hpi, body, 0)


def _attention(q, kt, v, *, tq):
    bsz, nh, seq, _ = q.shape
    nq = seq // tq
    head_tile = lambda rows: pl.BlockSpec((None, nh, rows, HEAD_TILE), lambda b, i: (b, 0, i, 0))
    whole = lambda shape: pl.BlockSpec((None,) + shape, lambda b, i: (b, 0, 0, 0))
    return pl.pallas_call(
        functools.partial(_attn_kernel, tq=tq, nq=nq),
        out_shape=jax.ShapeDtypeStruct((bsz, nh, seq, HEAD_TILE), BF16),
        grid=(bsz, nq),
        in_specs=[head_tile(tq), whole((nh, HEAD_TILE, seq)), whole((nh, seq, HEAD_TILE))],
        out_specs=head_tile(tq),
        scratch_shapes=[pltpu.VMEM((tq, SCORE_SCRATCH_SEQS * seq), F32),
                        pltpu.VMEM((MAX_HEADS_IN_FLIGHT, tq, LANES), F32),
                        pltpu.VMEM((tq, SCORE_SCRATCH_SEQS * seq), BF16)],
        compiler_params=_params(("parallel", "arbitrary")),
        name="causal_attention",
    )(q, kt, v)


def _tiles(bsz, seq):
    tm = 512 if seq % 512 == 0 else seq
    tc = 16 if seq % 16 == 0 else seq
    tq = 256 if seq % 256 == 0 else seq
    return tm, tc, tq


def kernel(x, positions, mix_norm, ffn_norm, ffn_w_gate_up, ffn_w_down, ssm_w_in, ssm_lambda_re, ssm_lambda_im, ssm_log_step, ssm_b_re, ssm_b_im, ssm_c_re, ssm_c_im, ssm_d, ssm_w_glu, kv_in_norm, mla_w_kv_a, mla_kv_a_norm, mla_w_kv_b, mla_k_nope_norm, mla_k_rope_norm, mla_w_q_a, mla_q_a_norm, mla_w_q_b, mla_q_nope_norm, mla_q_rope_norm, mla_w_o):
    bsz, seq, d = x.shape
    n = bsz * seq
    depth = mix_norm.shape[0]
    n_ssm = ssm_w_in.shape[0]
    assert bsz % 8 == 0 and d % CH_TILE == 0
    tm, tc, tq = _tiles(bsz, seq)

    xs = x
    for i in range(n_ssm):
        a_re, a_im, br, bi = _discretise(ssm_lambda_re[i], ssm_lambda_im[i], ssm_log_step[i],
                                         ssm_b_re[i], ssm_b_im[i])
        b_blk, c_blk = _block_diag_weights(br, bi, ssm_c_re[i], ssm_c_im[i])
        xs = _s5_layer(xs, mix_norm[i], ssm_w_in[i], b_blk, a_re, a_im, c_blk, ssm_d[i],
                       ssm_w_glu[i], tc=tc, batch_major_in=(i == 0))
        xs = _ffn_layer(xs.reshape(n, d), ffn_norm[i], ffn_w_gate_up[i], ffn_w_down[i], tm=tm,
                        batch_major_out=bsz if i == n_ssm - 1 else None)
        if i < n_ssm - 1:
            xs = xs.reshape(seq, bsz, d)
    x2 = xs.reshape(n, d)

    pos2 = positions.reshape(n, 1)
    inv_freq = ROPE_THETA ** (-jnp.arange(0, QK_ROPE, 2, dtype=F32) / QK_ROPE)
    k, v, cos_t, sin_t = _kv_proj(x2, pos2, inv_freq, kv_in_norm, mla_w_kv_a, mla_kv_a_norm,
                                  mla_w_kv_b, mla_k_nope_norm, mla_k_rope_norm, tm=tm, bsz=bsz)
    for layer in range(n_ssm, depth):
        j = layer - n_ssm
        q = _q_proj(x2, cos_t, sin_t, mix_norm[layer], mla_w_q_a[j], mla_q_a_norm[j],
                    mla_w_q_b[j], mla_q_nope_norm[j], mla_q_rope_norm[j], tm=tm, bsz=bsz)
        o = _attention(q, k, v, tq=tq)
        x2 = _ffn_layer(x2, ffn_norm[layer], ffn_w_gate_up[layer], ffn_w_down[layer], tm=tm,
                        attn=o, w_o=mla_w_o[j])
    return x2.reshape(bsz, seq, d)
```

```python
import functools
import math

import jax
import jax.numpy as jnp
from jax import lax
from jax.experimental import pallas as pl
from jax.experimental.pallas import tpu as pltpu

F32 = jnp.float32
BF16 = jnp.bfloat16

EPS = 1e-6
NEG_INF = -1e30
ROPE_THETA = 10000.0

SSM_GROUP = 16
SSM_STATE = 64
N_HEADS = 16
QK_NOPE = 64
QK_ROPE = 32
HALF_ROPE = QK_ROPE // 2
QK_DIM = QK_NOPE + QK_ROPE
V_DIM = 64
KV_LORA = 256

LANES = 128
MXU_TILE = 256
VMEM_LIMIT = 56 * 1024 * 1024

HEAD_TILE = LANES
CH_TILE = MXU_TILE
GROUPS_PER_TILE = CH_TILE // SSM_GROUP
STATES_PER_TILE = GROUPS_PER_TILE * SSM_STATE


def _rms(x, gain):
    ms = jnp.mean(x * x, axis=-1, keepdims=True)
    return x * lax.rsqrt(ms + EPS) * gain


def _const_spec(shape):
    nd = len(shape)
    return pl.BlockSpec(shape, lambda *_: (0,) * nd, pipeline_mode=pl.Buffered(1))


def _params(semantics):
    return pltpu.CompilerParams(dimension_semantics=semantics, vmem_limit_bytes=VMEM_LIMIT)


def _disc_kernel(lr_ref, li_ref, ls_ref, bre_ref, bim_ref, are_ref, aim_ref, br_ref, bi_ref):
    lr = lr_ref[...]
    li = li_ref[...]
    step = jnp.exp(ls_ref[...])
    decay = jnp.exp(lr * step)
    a_re = decay * jnp.cos(li * step)
    a_im = decay * jnp.sin(li * step)
    den = lr * lr + li * li
    f_re = ((a_re - 1.0) * lr + a_im * li) / den
    f_im = (a_im * lr - (a_re - 1.0) * li) / den
    b_re = bre_ref[...]
    b_im = bim_ref[...]
    are_ref[...] = a_re
    aim_ref[...] = a_im
    br_ref[...] = f_re * b_re - f_im * b_im
    bi_ref[...] = f_re * b_im + f_im * b_re


def _discretise(lam_re, lam_im, log_step, b_re, b_im):
    g, n = lam_re.shape
    k = b_re.shape[-1]
    gn = g * n
    col = lambda a: a.reshape(gn, 1)
    ls = jnp.broadcast_to(log_step[:, None], (g, n))
    outs = pl.pallas_call(
        _disc_kernel,
        out_shape=[jax.ShapeDtypeStruct((gn, 1), F32), jax.ShapeDtypeStruct((gn, 1), F32),
                   jax.ShapeDtypeStruct((gn, k), F32), jax.ShapeDtypeStruct((gn, k), F32)],
        name="s5_discretise",
    )(col(lam_re), col(lam_im), col(ls), b_re.reshape(gn, k), b_im.reshape(gn, k))
    a_re, a_im, br, bi = outs
    return a_re.reshape(g, n), a_im.reshape(g, n), br.reshape(g, n, k), bi.reshape(g, n, k)


def _block_diag_weights(br, bi, c_re, c_im):
    g, n, k = br.shape
    tiles = g // GROUPS_PER_TILE
    eye = jnp.eye(GROUPS_PER_TILE, dtype=F32)

    def bmat(b):
        b4 = b.reshape(tiles, GROUPS_PER_TILE, n, k)
        return jnp.einsum('qgnk,gh->qgkhn', b4, eye).reshape(tiles, CH_TILE, STATES_PER_TILE)

    def cmat(c):
        c4 = c.reshape(tiles, GROUPS_PER_TILE, k, n)
        return jnp.einsum('qgkn,gh->qgnhk', c4, eye).reshape(tiles, STATES_PER_TILE, CH_TILE)

    b_blk = jnp.concatenate([bmat(br), bmat(bi)], axis=-1).astype(BF16)
    c_blk = jnp.concatenate([cmat(c_re), -cmat(c_im)], axis=1).astype(BF16)
    return b_blk, c_blk


SCAN_LANES = 256


def _s5_kernel(x_ref, g_ref, win_ref, b_ref, are_ref, aim_ref, c_ref, d_ref, wglu_ref,
               o_ref, st_ref, bu_ref, z_ref, *, tc, nb, d_model, batch_major_in):
    rows = tc * nb
    tiles = d_model // CH_TILE

    @pl.when(pl.program_id(0) == 0)
    def _():
        st_ref[...] = jnp.zeros_like(st_ref)

    x = x_ref[...]
    if batch_major_in:
        x = pltpu.einshape("btd->tbd", x)
    x = x.reshape(rows, d_model)
    h = _rms(x, g_ref[...]).astype(BF16)
    u = jnp.dot(h, win_ref[...], preferred_element_type=F32)
    u_bf = u.astype(BF16)

    for q in range(tiles):
        ch = slice(q * CH_TILE, (q + 1) * CH_TILE)
        bu = bu_ref
        bu[...] = jnp.dot(u_bf[:, ch], b_ref[q], preferred_element_type=F32)

        for c in range(STATES_PER_TILE // SCAN_LANES):
            re = slice(c * SCAN_LANES, (c + 1) * SCAN_LANES)
            im = slice(STATES_PER_TILE + c * SCAN_LANES, STATES_PER_TILE + (c + 1) * SCAN_LANES)
            ar = jnp.broadcast_to(are_ref[q, :, re], (nb, SCAN_LANES))
            ai = jnp.broadcast_to(aim_ref[q, :, re], (nb, SCAN_LANES))

            def step(t, carry, re=re, im=im, ar=ar, ai=ai, bu=bu):
                sr, si = carry
                r0 = pl.multiple_of(t * nb, nb)
                nr = ar * sr - ai * si + bu[pl.ds(r0, nb), re]
                ni = ar * si + ai * sr + bu[pl.ds(r0, nb), im]
                bu[pl.ds(r0, nb), re] = nr
                bu[pl.ds(r0, nb), im] = ni
                return nr, ni

            sr, si = lax.fori_loop(0, tc, step, (st_ref[q, :, re], st_ref[q, :, im]), unroll=True)
            st_ref[q, :, re] = sr
            st_ref[q, :, im] = si

        y = jnp.dot(bu[...].astype(BF16), c_ref[q], preferred_element_type=F32)
        y = y + d_ref[:, ch] * u[:, ch]
        z_ref[:, ch] = jax.nn.gelu(y).astype(BF16)

    glu = jnp.dot(z_ref[...], wglu_ref[...], preferred_element_type=F32)
    out = x + glu[:, :d_model] * jax.nn.sigmoid(glu[:, d_model:])
    o_ref[...] = out.reshape(tc, nb, d_model)


def _s5_layer(xin, gain, w_in, b_blk, a_re, a_im, c_blk, d_skip, w_glu, *, tc, batch_major_in):
    if batch_major_in:
        nb, seq, d = xin.shape
        x_spec = pl.BlockSpec((nb, tc, d), lambda i: (0, i, 0))
    else:
        seq, nb, d = xin.shape
        x_spec = pl.BlockSpec((tc, nb, d), lambda i: (i, 0, 0))
    tiles = d // CH_TILE
    rows = tc * nb
    kern = functools.partial(_s5_kernel, tc=tc, nb=nb, d_model=d, batch_major_in=batch_major_in)
    return pl.pallas_call(
        kern,
        out_shape=jax.ShapeDtypeStruct((seq, nb, d), F32),
        grid=(seq // tc,),
        in_specs=[
            x_spec,
            _const_spec((1, d)),
            _const_spec((d, d)),
            _const_spec((tiles, CH_TILE, 2 * STATES_PER_TILE)),
            _const_spec((tiles, 1, STATES_PER_TILE)),
            _const_spec((tiles, 1, STATES_PER_TILE)),
            _const_spec((tiles, 2 * STATES_PER_TILE, CH_TILE)),
            _const_spec((1, d)),
            _const_spec((d, 2 * d)),
        ],
        out_specs=pl.BlockSpec((tc, nb, d), lambda i: (i, 0, 0)),
        scratch_shapes=[
            pltpu.VMEM((tiles, nb, 2 * STATES_PER_TILE), F32),
            pltpu.VMEM((rows, 2 * STATES_PER_TILE), F32),
            pltpu.VMEM((rows, d), BF16),
        ],
        compiler_params=_params(("arbitrary",)),
        name="s5_layer",
    )(xin, gain.reshape(1, d), w_in.astype(BF16), b_blk,
      a_re.reshape(tiles, 1, STATES_PER_TILE), a_im.reshape(tiles, 1, STATES_PER_TILE),
      c_blk, d_skip.reshape(1, d), w_glu.astype(BF16))


FF_CHUNK = MXU_TILE


def _ffn_body(x, g_ref, wgu_ref, wd_ref, d_ff):
    h = _rms(x, g_ref[...]).astype(BF16)
    acc = jnp.zeros(x.shape, F32)
    for c in range(d_ff // FF_CHUNK):
        gate = jnp.dot(h, wgu_ref[:, c * FF_CHUNK:(c + 1) * FF_CHUNK], preferred_element_type=F32)
        up = jnp.dot(h, wgu_ref[:, d_ff + c * FF_CHUNK:d_ff + (c + 1) * FF_CHUNK],
                     preferred_element_type=F32)
        act = (jax.nn.silu(gate) * up).astype(BF16)
        acc = acc + jnp.dot(act, wd_ref[c * FF_CHUNK:(c + 1) * FF_CHUNK, :],
                            preferred_element_type=F32)
    return x + acc


def _ffn_kernel(x_ref, g_ref, wgu_ref, wd_ref, o_ref, *, d_ff):
    out = _ffn_body(x_ref[...], g_ref, wgu_ref, wd_ref, d_ff)
    if len(o_ref.shape) == 3:
        nb, steps, d = o_ref.shape
        out = pltpu.einshape("tbd->btd", out.reshape(steps, nb, d))
    o_ref[...] = out


def _proj_ffn_kernel(x_ref, a_ref, wo_ref, g_ref, wgu_ref, wd_ref, o_ref, *, d_ff):
    lane = lax.broadcasted_iota(jnp.int32, a_ref.shape[1:], 1)
    attn = jnp.concatenate([jnp.where(lane < V_DIM, a_ref[2 * p], a_ref[2 * p + 1])
                            for p in range(a_ref.shape[0] // 2)], axis=-1)
    x = x_ref[...] + jnp.dot(attn, wo_ref[...], preferred_element_type=F32)
    o_ref[...] = _ffn_body(x, g_ref, wgu_ref, wd_ref, d_ff)


def _ffn_layer(x2, gain, w_gate_up, w_down, *, tm, attn=None, w_o=None, batch_major_out=None):
    n, d = x2.shape
    d_ff = w_down.shape[0]
    row_spec = pl.BlockSpec((tm, d), lambda i: (i, 0))
    out_shape, out_spec = (n, d), row_spec
    if batch_major_out is not None:
        nb = batch_major_out
        out_shape = (nb, n // nb, d)
        out_spec = pl.BlockSpec((nb, tm // nb, d), lambda i: (0, i, 0))
    w_specs = [_const_spec((1, d)), _const_spec((d, 2 * d_ff)), _const_spec((d_ff, d))]
    w_args = (gain.reshape(1, d), w_gate_up.astype(BF16), w_down.astype(BF16))
    if attn is None:
        kern, in_specs, args = _ffn_kernel, [row_spec] + w_specs, (x2,) + w_args
    else:
        _, heads, seq, width = attn.shape
        per_b = seq // tm
        kern = _proj_ffn_kernel
        attn_spec = pl.BlockSpec((None, heads, tm, width), lambda i: (i // per_b, 0, i % per_b, 0))
        in_specs = [row_spec, attn_spec, _const_spec((heads // 2 * width, d))] + w_specs
        args = (x2, attn, w_o.astype(BF16)) + w_args
    return pl.pallas_call(
        functools.partial(kern, d_ff=d_ff),
        out_shape=jax.ShapeDtypeStruct(out_shape, F32),
        grid=(n // tm,),
        in_specs=in_specs,
        out_specs=out_spec,
        compiler_params=_params(("parallel",)),
        name="ffn_layer",
    )(*args)


def _rope_tables(pos_ref, freq_ref):
    ang = pos_ref[...].astype(F32) * freq_ref[...]
    return jnp.cos(ang), jnp.sin(ang)


def _rotate(tn, cos_t, sin_t, swap):
    partner = jnp.dot(tn.astype(BF16), swap, preferred_element_type=F32)
    return tn * cos_t + partner * sin_t


def _rope_swap_matrix(tiles=1):
    src = jnp.arange(HEAD_TILE)[:, None]
    dst = jnp.arange(HEAD_TILE)[None, :]
    x1_col = (dst >= QK_NOPE) & (dst < QK_NOPE + HALF_ROPE)
    x2_col = (dst >= QK_NOPE + HALF_ROPE) & (dst < QK_DIM)
    p = (jnp.where(x1_col & (src == dst + HALF_ROPE), -1.0, 0.0)
         + jnp.where(x2_col & (src == dst - HALF_ROPE), 1.0, 0.0))
    return jnp.kron(jnp.eye(tiles, dtype=F32), p).astype(BF16)


def _segment_mean_matrix(tiles=1):
    idx = jnp.arange(HEAD_TILE)
    seg = jnp.where(idx < QK_NOPE, 0, jnp.where(idx < QK_DIM, 1, 2))
    same = (seg[:, None] == seg[None, :]) & (seg[:, None] < 2)
    width = jnp.where(seg == 0, QK_NOPE, QK_ROPE).astype(F32)
    e = jnp.where(same, 1.0 / width[None, :], 0.0)
    return jnp.kron(jnp.eye(tiles, dtype=F32), e).astype(BF16)


def _head_tile_freq(inv_freq):
    z = jnp.zeros((QK_NOPE,), F32)
    pad = jnp.zeros((HEAD_TILE - QK_DIM,), F32)
    return jnp.concatenate([z, inv_freq, inv_freq, pad]).reshape(1, HEAD_TILE)


def _head_tile_gain(nope_gain, rope_gain):
    pad = jnp.zeros((HEAD_TILE - QK_DIM,), F32)
    return jnp.concatenate([nope_gain, rope_gain, pad]).reshape(1, HEAD_TILE)


def _kv_kernel(x_ref, pos_ref, freq_ref, g_ref, wa_ref, an_ref, wk_ref, wv_ref, e_ref, sw_ref,
               gt_ref, gc_ref, k_ref, v_ref, cos_ref, sin_ref):
    h = _rms(x_ref[...], g_ref[...]).astype(BF16)
    ckv = jnp.dot(h, wa_ref[...], preferred_element_type=F32)
    c_kv = _rms(ckv[:, :KV_LORA], an_ref[...]).astype(BF16)
    cos_t, sin_t = _rope_tables(pos_ref, freq_ref)
    cos_ref[...] = cos_t
    sin_ref[...] = sin_t
    gt = gt_ref[...]
    e = e_ref[...]

    kr = ckv[:, KV_LORA:]
    ms = jnp.dot((kr * kr).astype(BF16), e, preferred_element_type=F32)
    kr = _rotate(kr * lax.rsqrt(ms + EPS) * gt, cos_t, sin_t, sw_ref[...])
    lane = lax.broadcasted_iota(jnp.int32, kr.shape, 1)
    tm = kr.shape[0]

    kr_t = kr.T
    kn_t = lax.dot_general(wk_ref[...], c_kv, (((1,), (1,)), ((), ())),
                           preferred_element_type=F32)
    rowid = lax.broadcasted_iota(jnp.int32, (HEAD_TILE, tm), 0)
    gcol = jnp.broadcast_to(gc_ref[...], (HEAD_TILE, tm))
    for hd in range(N_HEADS):
        t = kn_t[hd * HEAD_TILE:(hd + 1) * HEAD_TILE, :]
        nope = t[:QK_NOPE]
        ms = jnp.sum(nope * nope, axis=0, keepdims=True) * (1.0 / QK_NOPE)
        tn = t * lax.rsqrt(ms + EPS) * gcol
        k_ref[hd] = jnp.where(rowid < QK_NOPE, tn, kr_t).astype(BF16)
    v = jnp.dot(c_kv, wv_ref[...], preferred_element_type=F32)
    for hd in range(N_HEADS):
        tile = v[:, hd * HEAD_TILE:(hd + 1) * HEAD_TILE]
        v_ref[hd] = jnp.where(lane == _ones_lane(hd % 2), 1.0, tile).astype(BF16)


def _ones_lane(parity):
    return V_DIM if parity == 0 else HEAD_TILE - V_DIM - 1


def _head_major_spec(heads, tm, seq):
    per_b = seq // tm
    return pl.BlockSpec((None, heads, tm, HEAD_TILE), lambda i: (i // per_b, 0, i % per_b, 0))


def _kv_proj(x2, pos2, inv_freq, gain, w_kv_a, kv_a_norm, w_kv_b, k_nope_norm, k_rope_norm, *,
             tm, bsz):
    n, d = x2.shape
    seq = n // bsz
    wa_rope = jnp.pad(w_kv_a[:, KV_LORA:], ((0, 0), (QK_NOPE, HEAD_TILE - QK_DIM)))
    wa = jnp.concatenate([w_kv_a[:, :KV_LORA], wa_rope], axis=1).astype(BF16)
    wb = w_kv_b.reshape(KV_LORA, N_HEADS, QK_NOPE + V_DIM)
    wk = jnp.pad(wb[:, :, :QK_NOPE], ((0, 0), (0, 0), (0, HEAD_TILE - QK_NOPE)))
    wk = wk.reshape(KV_LORA, N_HEADS * HEAD_TILE).T.astype(BF16)
    wv = wb[:, :, QK_NOPE:]
    odd_head = (jnp.arange(N_HEADS) % 2 == 1)[None, :, None]
    wv = jnp.where(odd_head, jnp.pad(wv, ((0, 0), (0, 0), (HEAD_TILE - V_DIM, 0))),
                   jnp.pad(wv, ((0, 0), (0, 0), (0, HEAD_TILE - V_DIM))))
    wv = wv.reshape(KV_LORA, N_HEADS * HEAD_TILE).astype(BF16)
    gain_tile = _head_tile_gain(k_nope_norm, k_rope_norm)
    row = lambda w: pl.BlockSpec((tm, w), lambda i: (i, 0))
    per_b = seq // tm
    kt_spec = pl.BlockSpec((None, N_HEADS, HEAD_TILE, tm), lambda i: (i // per_b, 0, 0, i % per_b))
    return pl.pallas_call(
        _kv_kernel,
        out_shape=[jax.ShapeDtypeStruct((bsz, N_HEADS, HEAD_TILE, seq), BF16),
                   jax.ShapeDtypeStruct((bsz, N_HEADS, seq, HEAD_TILE), BF16),
                   jax.ShapeDtypeStruct((n, HEAD_TILE), F32),
                   jax.ShapeDtypeStruct((n, HEAD_TILE), F32)],
        grid=(n // tm,),
        in_specs=[row(d), row(1), _const_spec((1, HEAD_TILE)), _const_spec((1, d)),
                  _const_spec(wa.shape), _const_spec((1, KV_LORA)), _const_spec(wk.shape),
                  _const_spec(wv.shape), _const_spec((HEAD_TILE, HEAD_TILE)),
                  _const_spec((HEAD_TILE, HEAD_TILE)),
                  _const_spec((1, HEAD_TILE)), _const_spec((HEAD_TILE, 1))],
        out_specs=[kt_spec, _head_major_spec(N_HEADS, tm, seq), row(HEAD_TILE), row(HEAD_TILE)],
        compiler_params=_params(("parallel",)),
        name="kv_proj",
    )(x2, pos2, _head_tile_freq(inv_freq), gain.reshape(1, d), wa, kv_a_norm.reshape(1, KV_LORA),
      wk, wv, _segment_mean_matrix(), _rope_swap_matrix(), gain_tile,
      gain_tile.reshape(HEAD_TILE, 1))


def _q_kernel(x_ref, cos_ref, sin_ref, g_ref, wa_ref, an_ref, wb_ref, e_ref, sw_ref, gt_ref, q_ref, *,
              scale):
    h = _rms(x_ref[...], g_ref[...]).astype(BF16)
    c_q = _rms(jnp.dot(h, wa_ref[...], preferred_element_type=F32), an_ref[...]).astype(BF16)
    q = jnp.dot(c_q, wb_ref[...], preferred_element_type=F32)
    cos2 = jnp.concatenate([cos_ref[...] * scale] * 2, axis=-1)
    sin2 = jnp.concatenate([sin_ref[...] * scale] * 2, axis=-1)
    gt2 = jnp.concatenate([gt_ref[...]] * 2, axis=-1)
    e = e_ref[...]
    swap = sw_ref[...]
    for pair in range(N_HEADS // 2):
        t2 = q[:, 2 * pair * HEAD_TILE:(2 * pair + 2) * HEAD_TILE]
        ms2 = jnp.dot((t2 * t2).astype(BF16), e, preferred_element_type=F32)
        out = _rotate(t2 * lax.rsqrt(ms2 + EPS) * gt2, cos2, sin2, swap).astype(BF16)
        q_ref[2 * pair] = out[:, :HEAD_TILE]
        q_ref[2 * pair + 1] = out[:, HEAD_TILE:]


def _q_proj(x2, cos_t, sin_t, gain, w_q_a, q_a_norm, w_q_b, q_nope_norm, q_rope_norm, *, tm, bsz):
    n, d = x2.shape
    seq = n // bsz
    rank = w_q_a.shape[1]
    wb = w_q_b.reshape(rank, N_HEADS, QK_DIM)
    wb = jnp.pad(wb, ((0, 0), (0, 0), (0, HEAD_TILE - QK_DIM)))
    wb = wb.reshape(rank, N_HEADS * HEAD_TILE).astype(BF16)
    row = lambda w: pl.BlockSpec((tm, w), lambda i: (i, 0))
    return pl.pallas_call(
        functools.partial(_q_kernel, scale=math.log2(math.e) / math.sqrt(QK_DIM)),
        out_shape=jax.ShapeDtypeStruct((bsz, N_HEADS, seq, HEAD_TILE), BF16),
        grid=(n // tm,),
        in_specs=[row(d), row(HEAD_TILE), row(HEAD_TILE), _const_spec((1, d)),
                  _const_spec((d, rank)), _const_spec((1, rank)), _const_spec(wb.shape),
                  _const_spec((2 * HEAD_TILE, 2 * HEAD_TILE)),
                  _const_spec((2 * HEAD_TILE, 2 * HEAD_TILE)), _const_spec((1, HEAD_TILE))],
        out_specs=_head_major_spec(N_HEADS, tm, seq),
        compiler_params=_params(("parallel",)),
        name="q_proj",
    )(x2, cos_t, sin_t, gain.reshape(1, d), w_q_a.astype(BF16),
      q_a_norm.reshape(1, rank), wb, _segment_mean_matrix(2), _rope_swap_matrix(2),
      _head_tile_gain(q_nope_norm, q_rope_norm))


MAX_HEADS_IN_FLIGHT = 8
SCORE_SCRATCH_SEQS = 4


def _attn_kernel(q_ref, kt_ref, v_ref, o_ref, s_ref, m_ref, p_ref, *, tq, nq):
    i = pl.program_id(1)
    nh = q_ref.shape[0]
    row = lax.broadcasted_iota(jnp.int32, (tq, tq), 0)
    col = lax.broadcasted_iota(jnp.int32, (tq, tq), 1)
    causal = col <= row
    lane_tiles = tq // LANES

    def stage_a(h, base, ms, nk):
        qh = q_ref[h]
        mx = None
        for j in range(nk):
            s = jnp.dot(qh, kt_ref[h, :, j * tq:(j + 1) * tq], preferred_element_type=F32)
            if j == nk - 1:
                s = jnp.where(causal, s, NEG_INF)
            s_ref[:, base + j * tq:base + (j + 1) * tq] = s
            for t in range(lane_tiles):
                f = s[:, t * LANES:(t + 1) * LANES]
                mx = f if mx is None else jnp.maximum(mx, f)
        m = jnp.max(mx, axis=-1, keepdims=True)
        m_ref[ms] = jnp.broadcast_to(m, (tq, LANES))

    def stage_b(h, base, ms, nk):
        m = m_ref[ms]
        m = jnp.concatenate([m] * lane_tiles, axis=-1)
        for j in range(nk):
            cols = slice(base + j * tq, base + (j + 1) * tq)
            p_ref[:, cols] = jnp.exp2((s_ref[:, cols] - m).astype(BF16))
        acc = jnp.dot(p_ref[:, base:base + nk * tq], v_ref[h, :nk * tq, :],
                      preferred_element_type=F32)
        odd = (h % 2) == 1
        denom = jnp.where(odd, acc[:, _ones_lane(1):_ones_lane(1) + 1],
                          acc[:, _ones_lane(0):_ones_lane(0) + 1])
        o_ref[h] = (acc / denom).astype(BF16)

    for c in range(nq):
        @pl.when(i == c)
        def _(c=c):
            nk = c + 1
            hpi = min(m_ref.shape[0], s_ref.shape[1] // (nk * tq))
            hpi = 1 << (hpi.bit_length() - 1)

            def body(it, carry):
                for k in range(hpi):
                    stage_a(it * hpi + k, k * nk * tq, k, nk)
                for k in range(hpi):
                    stage_b(it * hpi + k, k * nk * tq, k, nk)
                return carry
            lax.fori_loop(0, nh // hpi, body, 0)


def _attention(q, kt, v, *, tq):
    bsz, nh, seq, _ = q.shape
    nq = seq // tq
    head_tile = lambda rows: pl.BlockSpec((None, nh, rows, HEAD_TILE), lambda b, i: (b, 0, i, 0))
    whole = lambda shape: pl.BlockSpec((None,) + shape, lambda b, i: (b, 0, 0, 0))
    return pl.pallas_call(
        functools.partial(_attn_kernel, tq=tq, nq=nq),
        out_shape=jax.ShapeDtypeStruct((bsz, nh, seq, HEAD_TILE), BF16),
        grid=(bsz, nq),
        in_specs=[head_tile(tq), whole((nh, HEAD_TILE, seq)), whole((nh, seq, HEAD_TILE))],
        out_specs=head_tile(tq),
        scratch_shapes=[pltpu.VMEM((tq, SCORE_SCRATCH_SEQS * seq), F32),
                        pltpu.VMEM((MAX_HEADS_IN_FLIGHT, tq, LANES), F32),
                        pltpu.VMEM((tq, SCORE_SCRATCH_SEQS * seq), BF16)],
        compiler_params=_params(("parallel", "arbitrary")),
        name="causal_attention",
    )(q, kt, v)


def _tiles(bsz, seq):
    tm = 512 if seq % 512 == 0 else seq
    tc = 16 if seq % 16 == 0 else seq
    tq = 256 if seq % 256 == 0 else seq
    return tm, tc, tq


def kernel(x, positions, mix_norm, ffn_norm, ffn_w_gate_up, ffn_w_down, ssm_w_in, ssm_lambda_re, ssm_lambda_im, ssm_log_step, ssm_b_re, ssm_b_im, ssm_c_re, ssm_c_im, ssm_d, ssm_w_glu, kv_in_norm, mla_w_kv_a, mla_kv_a_norm, mla_w_kv_b, mla_k_nope_norm, mla_k_rope_norm, mla_w_q_a, mla_q_a_norm, mla_w_q_b, mla_q_nope_norm, mla_q_rope_norm, mla_w_o):
    bsz, seq, d = x.shape
    n = bsz * seq
    depth = mix_norm.shape[0]
    n_ssm = ssm_w_in.shape[0]
    assert bsz % 8 == 0 and d % CH_TILE == 0
    tm, tc, tq = _tiles(bsz, seq)

    xs = x
    for i in range(n_ssm):
        a_re, a_im, br, bi = _discretise(ssm_lambda_re[i], ssm_lambda_im[i], ssm_log_step[i],
                                         ssm_b_re[i], ssm_b_im[i])
        b_blk, c_blk = _block_diag_weights(br, bi, ssm_c_re[i], ssm_c_im[i])
        xs = _s5_layer(xs, mix_norm[i], ssm_w_in[i], b_blk, a_re, a_im, c_blk, ssm_d[i],
                       ssm_w_glu[i], tc=tc, batch_major_in=(i == 0))
        xs = _ffn_layer(xs.reshape(n, d), ffn_norm[i], ffn_w_gate_up[i], ffn_w_down[i], tm=tm,
                        batch_major_out=bsz if i == n_ssm - 1 else None)
        if i < n_ssm - 1:
            xs = xs.reshape(seq, bsz, d)
    x2 = xs.reshape(n, d)

    pos2 = positions.reshape(n, 1)
    inv_freq = ROPE_THETA ** (-jnp.arange(0, QK_ROPE, 2, dtype=F32) / QK_ROPE)
    k, v, cos_t, sin_t = _kv_proj(x2, pos2, inv_freq, kv_in_norm, mla_w_kv_a, mla_kv_a_norm,
                                  mla_w_kv_b, mla_k_nope_norm, mla_k_rope_norm, tm=tm, bsz=bsz)
    for layer in range(n_ssm, depth):
        j = layer - n_ssm
        q = _q_proj(x2, cos_t, sin_t, mix_norm[layer], mla_w_q_a[j], mla_q_a_norm[j],
                    mla_w_q_b[j], mla_q_nope_norm[j], mla_q_rope_norm[j], tm=tm, bsz=bsz)
        o = _attention(q, k, v, tq=tq)
        x2 = _ffn_layer(x2, ffn_norm[layer], ffn_w_gate_up[layer], ffn_w_down[layer], tm=tm,
                        attn=o, w_o=mla_w_o[j])
    return x2.reshape(bsz, seq, d)
```

```python
import functools
import math

import jax
import jax.numpy as jnp
from jax import lax
from jax.experimental import pallas as pl
from jax.experimental.pallas import tpu as pltpu

F32 = jnp.float32
BF16 = jnp.bfloat16

EPS = 1e-6
NEG_INF = -1e30
ROPE_THETA = 10000.0

SSM_GROUP = 16
SSM_STATE = 64
N_HEADS = 16
QK_NOPE = 64
QK_ROPE = 32
HALF_ROPE = QK_ROPE // 2
QK_DIM = QK_NOPE + QK_ROPE
V_DIM = 64
KV_LORA = 256

LANES = 128
MXU_TILE = 256
VMEM_LIMIT = 56 * 1024 * 1024

HEAD_TILE = LANES
CH_TILE = MXU_TILE
GROUPS_PER_TILE = CH_TILE // SSM_GROUP
STATES_PER_TILE = GROUPS_PER_TILE * SSM_STATE
SCAN_LANES = LANES


def _rms(x, gain):
    ms = jnp.mean(x * x, axis=-1, keepdims=True)
    return x * lax.rsqrt(ms + EPS) * gain


def _const_spec(shape):
    nd = len(shape)
    return pl.BlockSpec(shape, lambda *_: (0,) * nd, pipeline_mode=pl.Buffered(1))


def _params(semantics):
    return pltpu.CompilerParams(dimension_semantics=semantics, vmem_limit_bytes=VMEM_LIMIT)


def _disc_kernel(lr_ref, li_ref, ls_ref, bre_ref, bim_ref, are_ref, aim_ref, br_ref, bi_ref):
    lr = lr_ref[...]
    li = li_ref[...]
    step = jnp.exp(ls_ref[...])
    decay = jnp.exp(lr * step)
    a_re = decay * jnp.cos(li * step)
    a_im = decay * jnp.sin(li * step)
    den = lr * lr + li * li
    f_re = ((a_re - 1.0) * lr + a_im * li) / den
    f_im = (a_im * lr - (a_re - 1.0) * li) / den
    b_re = bre_ref[...]
    b_im = bim_ref[...]
    are_ref[...] = a_re
    aim_ref[...] = a_im
    br_ref[...] = f_re * b_re - f_im * b_im
    bi_ref[...] = f_re * b_im + f_im * b_re


def _discretise(lam_re, lam_im, log_step, b_re, b_im):
    g, n = lam_re.shape
    k = b_re.shape[-1]
    gn = g * n
    col = lambda a: a.reshape(gn, 1)
    ls = jnp.broadcast_to(log_step[:, None], (g, n))
    outs = pl.pallas_call(
        _disc_kernel,
        out_shape=[jax.ShapeDtypeStruct((gn, 1), F32), jax.ShapeDtypeStruct((gn, 1), F32),
                   jax.ShapeDtypeStruct((gn, k), F32), jax.ShapeDtypeStruct((gn, k), F32)],
        name="s5_discretise",
    )(col(lam_re), col(lam_im), col(ls), b_re.reshape(gn, k), b_im.reshape(gn, k))
    a_re, a_im, br, bi = outs
    return a_re.reshape(g, n), a_im.reshape(g, n), br.reshape(g, n, k), bi.reshape(g, n, k)


def _block_diag_weights(br, bi, c_re, c_im):
    g, n, k = br.shape
    tiles = g // GROUPS_PER_TILE
    eye = jnp.eye(GROUPS_PER_TILE, dtype=F32)

    def bmat(b):
        b4 = b.reshape(tiles, GROUPS_PER_TILE, n, k)
        return jnp.einsum('qgnk,gh->qgkhn', b4, eye).reshape(tiles, CH_TILE, STATES_PER_TILE)

    def cmat(c):
        c4 = c.reshape(tiles, GROUPS_PER_TILE, k, n)
        return jnp.einsum('qgkn,gh->qgnhk', c4, eye).reshape(tiles, STATES_PER_TILE, CH_TILE)

    chunks = STATES_PER_TILE // SCAN_LANES
    b_blk = jnp.stack([bmat(br).reshape(tiles, CH_TILE, chunks, SCAN_LANES),
                       bmat(bi).reshape(tiles, CH_TILE, chunks, SCAN_LANES)], axis=3)
    c_blk = jnp.stack([cmat(c_re).reshape(tiles, chunks, SCAN_LANES, CH_TILE),
                       -cmat(c_im).reshape(tiles, chunks, SCAN_LANES, CH_TILE)], axis=2)
    return (b_blk.reshape(tiles, CH_TILE, 2 * STATES_PER_TILE).astype(BF16),
            c_blk.reshape(tiles, 2 * STATES_PER_TILE, CH_TILE).astype(BF16))


def _s5_kernel(x_ref, g_ref, win_ref, b_ref, are_ref, aim_ref, c_ref, d_ref, wglu_ref,
               o_ref, st_ref, bu_ref, z_ref, *, tc, nb, d_model, batch_major_in):
    rows = tc * nb
    tiles = d_model // CH_TILE

    @pl.when(pl.program_id(0) == 0)
    def _():
        st_ref[...] = jnp.zeros_like(st_ref)

    x = x_ref[...]
    if batch_major_in:
        x = pltpu.einshape("btd->tbd", x)
    x = x.reshape(rows, d_model)
    h = _rms(x, g_ref[...]).astype(BF16)
    u = jnp.dot(h, win_ref[...], preferred_element_type=F32)
    u_bf = u.astype(BF16)

    for q in range(tiles):
        ch = slice(q * CH_TILE, (q + 1) * CH_TILE)
        y = d_ref[:, ch] * u[:, ch]
        for c in range(STATES_PER_TILE // SCAN_LANES):
            cols = slice(2 * c * SCAN_LANES, 2 * (c + 1) * SCAN_LANES)
            re = slice(2 * c * SCAN_LANES, (2 * c + 1) * SCAN_LANES)
            im = slice((2 * c + 1) * SCAN_LANES, (2 * c + 2) * SCAN_LANES)
            st = slice(c * SCAN_LANES, (c + 1) * SCAN_LANES)
            bu_ref[:, cols] = jnp.dot(u_bf[:, ch], b_ref[q, :, cols], preferred_element_type=F32)
            ar = jnp.broadcast_to(are_ref[q, :, st], (nb, SCAN_LANES))
            ai = jnp.broadcast_to(aim_ref[q, :, st], (nb, SCAN_LANES))
            sr, si = st_ref[q, :, re], st_ref[q, :, im]
            for t in range(tc):
                rws = slice(t * nb, (t + 1) * nb)
                nr = ar * sr - ai * si + bu_ref[rws, re]
                ni = ar * si + ai * sr + bu_ref[rws, im]
                bu_ref[rws, re] = nr
                bu_ref[rws, im] = ni
                sr, si = nr, ni
            st_ref[q, :, re] = sr
            st_ref[q, :, im] = si
            y = y + jnp.dot(bu_ref[:, cols].astype(BF16), c_ref[q, cols, :],
                            preferred_element_type=F32)
        z_ref[:, ch] = jax.nn.gelu(y).astype(BF16)

    glu = jnp.dot(z_ref[...], wglu_ref[...], preferred_element_type=F32)
    out = x + glu[:, :d_model] * jax.nn.sigmoid(glu[:, d_model:])
    o_ref[...] = out.reshape(tc, nb, d_model)


def _s5_layer(xin, gain, w_in, b_blk, a_re, a_im, c_blk, d_skip, w_glu, *, tc, batch_major_in):
    if batch_major_in:
        nb, seq, d = xin.shape
        x_spec = pl.BlockSpec((nb, tc, d), lambda i: (0, i, 0))
    else:
        seq, nb, d = xin.shape
        x_spec = pl.BlockSpec((tc, nb, d), lambda i: (i, 0, 0))
    tiles = d // CH_TILE
    rows = tc * nb
    kern = functools.partial(_s5_kernel, tc=tc, nb=nb, d_model=d, batch_major_in=batch_major_in)
    return pl.pallas_call(
        kern,
        out_shape=jax.ShapeDtypeStruct((seq, nb, d), F32),
        grid=(seq // tc,),
        in_specs=[
            x_spec,
            _const_spec((1, d)),
            _const_spec((d, d)),
            _const_spec((tiles, CH_TILE, 2 * STATES_PER_TILE)),
            _const_spec((tiles, 1, STATES_PER_TILE)),
            _const_spec((tiles, 1, STATES_PER_TILE)),
            _const_spec((tiles, 2 * STATES_PER_TILE, CH_TILE)),
            _const_spec((1, d)),
            _const_spec((d, 2 * d)),
        ],
        out_specs=pl.BlockSpec((tc, nb, d), lambda i: (i, 0, 0)),
        scratch_shapes=[
            pltpu.VMEM((tiles, nb, 2 * STATES_PER_TILE), F32),
            pltpu.VMEM((rows, 2 * STATES_PER_TILE), F32),
            pltpu.VMEM((rows, d), BF16),
        ],
        compiler_params=_params(("arbitrary",)),
        name="s5_layer",
    )(xin, gain.reshape(1, d), w_in.astype(BF16), b_blk,
      a_re.reshape(tiles, 1, STATES_PER_TILE), a_im.reshape(tiles, 1, STATES_PER_TILE),
      c_blk, d_skip.reshape(1, d), w_glu.astype(BF16))


FF_CHUNK = MXU_TILE


def _ffn_body(x, g_ref, wgu_ref, wd_ref, d_ff):
    h = _rms(x, g_ref[...]).astype(BF16)
    acc = jnp.zeros(x.shape, F32)
    for c in range(d_ff // FF_CHUNK):
        gate = jnp.dot(h, wgu_ref[:, c * FF_CHUNK:(c + 1) * FF_CHUNK], preferred_element_type=F32)
        up = jnp.dot(h, wgu_ref[:, d_ff + c * FF_CHUNK:d_ff + (c + 1) * FF_CHUNK],
                     preferred_element_type=F32)
        act = (jax.nn.silu(gate) * up).astype(BF16)
        acc = acc + jnp.dot(act, wd_ref[c * FF_CHUNK:(c + 1) * FF_CHUNK, :],
                            preferred_element_type=F32)
    return x + acc


def _ffn_kernel(x_ref, g_ref, wgu_ref, wd_ref, o_ref, *, d_ff):
    out = _ffn_body(x_ref[...], g_ref, wgu_ref, wd_ref, d_ff)
    if len(o_ref.shape) == 3:
        nb, steps, d = o_ref.shape
        out = pltpu.einshape("tbd->btd", out.reshape(steps, nb, d))
    o_ref[...] = out


def _proj_ffn_kernel(x_ref, a_ref, wo_ref, g_ref, wgu_ref, wd_ref, o_ref, *, d_ff):
    lane = lax.broadcasted_iota(jnp.int32, a_ref.shape[1:], 1)
    attn = jnp.concatenate([jnp.where(lane < V_DIM, a_ref[2 * p], a_ref[2 * p + 1])
                            for p in range(a_ref.shape[0] // 2)], axis=-1)
    x = x_ref[...] + jnp.dot(attn, wo_ref[...], preferred_element_type=F32)
    o_ref[...] = _ffn_body(x, g_ref, wgu_ref, wd_ref, d_ff)


def _ffn_layer(x2, gain, w_gate_up, w_down, *, tm, attn=None, w_o=None, batch_major_out=None):
    n, d = x2.shape
    d_ff = w_down.shape[0]
    row_spec = pl.BlockSpec((tm, d), lambda i: (i, 0))
    out_shape, out_spec = (n, d), row_spec
    if batch_major_out is not None:
        nb = batch_major_out
        out_shape = (nb, n // nb, d)
        out_spec = pl.BlockSpec((nb, tm // nb, d), lambda i: (0, i, 0))
    w_specs = [_const_spec((1, d)), _const_spec((d, 2 * d_ff)), _const_spec((d_ff, d))]
    w_args = (gain.reshape(1, d), w_gate_up.astype(BF16), w_down.astype(BF16))
    if attn is None:
        kern, in_specs, args = _ffn_kernel, [row_spec] + w_specs, (x2,) + w_args
    else:
        _, heads, seq, width = attn.shape
        per_b = seq // tm
        kern = _proj_ffn_kernel
        attn_spec = pl.BlockSpec((None, heads, tm, width), lambda i: (i // per_b, 0, i % per_b, 0))
        in_specs = [row_spec, attn_spec, _const_spec((heads // 2 * width, d))] + w_specs
        args = (x2, attn, w_o.astype(BF16)) + w_args
    return pl.pallas_call(
        functools.partial(kern, d_ff=d_ff),
        out_shape=jax.ShapeDtypeStruct(out_shape, F32),
        grid=(n // tm,),
        in_specs=in_specs,
        out_specs=out_spec,
        compiler_params=_params(("parallel",)),
        name="ffn_layer",
    )(*args)


def _rope_tables(pos_ref, freq_ref):
    ang = pos_ref[...].astype(F32) * freq_ref[...]
    return jnp.cos(ang), jnp.sin(ang)


def _rotate(tn, cos_t, sin_t, swap):
    partner = jnp.dot(tn.astype(BF16), swap, preferred_element_type=F32)
    return tn * cos_t + partner * sin_t


def _rope_swap_matrix(tiles=1):
    src = jnp.arange(HEAD_TILE)[:, None]
    dst = jnp.arange(HEAD_TILE)[None, :]
    x1_col = (dst >= QK_NOPE) & (dst < QK_NOPE + HALF_ROPE)
    x2_col = (dst >= QK_NOPE + HALF_ROPE) & (dst < QK_DIM)
    p = (jnp.where(x1_col & (src == dst + HALF_ROPE), -1.0, 0.0)
         + jnp.where(x2_col & (src == dst - HALF_ROPE), 1.0, 0.0))
    return jnp.kron(jnp.eye(tiles, dtype=F32), p).astype(BF16)


def _segment_mean_matrix(tiles=1):
    idx = jnp.arange(HEAD_TILE)
    seg = jnp.where(idx < QK_NOPE, 0, jnp.where(idx < QK_DIM, 1, 2))
    same = (seg[:, None] == seg[None, :]) & (seg[:, None] < 2)
    width = jnp.where(seg == 0, QK_NOPE, QK_ROPE).astype(F32)
    e = jnp.where(same, 1.0 / width[None, :], 0.0)
    return jnp.kron(jnp.eye(tiles, dtype=F32), e).astype(BF16)


def _head_tile_freq(inv_freq):
    z = jnp.zeros((QK_NOPE,), F32)
    pad = jnp.zeros((HEAD_TILE - QK_DIM,), F32)
    return jnp.concatenate([z, inv_freq, inv_freq, pad]).reshape(1, HEAD_TILE)


def _head_tile_gain(nope_gain, rope_gain):
    pad = jnp.zeros((HEAD_TILE - QK_DIM,), F32)
    return jnp.concatenate([nope_gain, rope_gain, pad]).reshape(1, HEAD_TILE)


def _kv_kernel(x_ref, pos_ref, freq_ref, g_ref, wa_ref, an_ref, wk_ref, wv_ref, e_ref, sw_ref,
               gt_ref, gc_ref, k_ref, v_ref, cos_ref, sin_ref):
    h = _rms(x_ref[...], g_ref[...]).astype(BF16)
    ckv = jnp.dot(h, wa_ref[...], preferred_element_type=F32)
    c_kv = _rms(ckv[:, :KV_LORA], an_ref[...]).astype(BF16)
    cos_t, sin_t = _rope_tables(pos_ref, freq_ref)
    cos_ref[...] = cos_t
    sin_ref[...] = sin_t
    gt = gt_ref[...]
    e = e_ref[...]

    kr = ckv[:, KV_LORA:]
    ms = jnp.dot((kr * kr).astype(BF16), e, preferred_element_type=F32)
    kr = _rotate(kr * lax.rsqrt(ms + EPS) * gt, cos_t, sin_t, sw_ref[...])
    lane = lax.broadcasted_iota(jnp.int32, kr.shape, 1)
    tm = kr.shape[0]

    kr_t = kr.T
    kn_t = lax.dot_general(wk_ref[...], c_kv, (((1,), (1,)), ((), ())),
                           preferred_element_type=F32)
    rowid = lax.broadcasted_iota(jnp.int32, (HEAD_TILE, tm), 0)
    gcol = jnp.broadcast_to(gc_ref[...], (HEAD_TILE, tm))
    for hd in range(N_HEADS):
        t = kn_t[hd * HEAD_TILE:(hd + 1) * HEAD_TILE, :]
        nope = t[:QK_NOPE]
        ms = jnp.sum(nope * nope, axis=0, keepdims=True) * (1.0 / QK_NOPE)
        tn = t * lax.rsqrt(ms + EPS) * gcol
        k_ref[hd] = jnp.where(rowid < QK_NOPE, tn, kr_t).astype(BF16)
    v = jnp.dot(c_kv, wv_ref[...], preferred_element_type=F32)
    for hd in range(N_HEADS):
        tile = v[:, hd * HEAD_TILE:(hd + 1) * HEAD_TILE]
        v_ref[hd] = jnp.where(lane == _ones_lane(hd % 2), 1.0, tile).astype(BF16)


def _ones_lane(parity):
    return V_DIM if parity == 0 else HEAD_TILE - V_DIM - 1


def _head_major_spec(heads, tm, seq):
    per_b = seq // tm
    return pl.BlockSpec((None, heads, tm, HEAD_TILE), lambda i: (i // per_b, 0, i % per_b, 0))


def _kv_proj(x2, pos2, inv_freq, gain, w_kv_a, kv_a_norm, w_kv_b, k_nope_norm, k_rope_norm, *,
             tm, bsz):
    n, d = x2.shape
    seq = n // bsz
    wa_rope = jnp.pad(w_kv_a[:, KV_LORA:], ((0, 0), (QK_NOPE, HEAD_TILE - QK_DIM)))
    wa = jnp.concatenate([w_kv_a[:, :KV_LORA], wa_rope], axis=1).astype(BF16)
    wb = w_kv_b.reshape(KV_LORA, N_HEADS, QK_NOPE + V_DIM)
    wk = jnp.pad(wb[:, :, :QK_NOPE], ((0, 0), (0, 0), (0, HEAD_TILE - QK_NOPE)))
    wk = wk.reshape(KV_LORA, N_HEADS * HEAD_TILE).T.astype(BF16)
    wv = wb[:, :, QK_NOPE:]
    odd_head = (jnp.arange(N_HEADS) % 2 == 1)[None, :, None]
    wv = jnp.where(odd_head, jnp.pad(wv, ((0, 0), (0, 0), (HEAD_TILE - V_DIM, 0))),
                   jnp.pad(wv, ((0, 0), (0, 0), (0, HEAD_TILE - V_DIM))))
    wv = wv.reshape(KV_LORA, N_HEADS * HEAD_TILE).astype(BF16)
    gain_tile = _head_tile_gain(k_nope_norm, k_rope_norm)
    row = lambda w: pl.BlockSpec((tm, w), lambda i: (i, 0))
    per_b = seq // tm
    kt_spec = pl.BlockSpec((None, N_HEADS, HEAD_TILE, tm), lambda i: (i // per_b, 0, 0, i % per_b))
    return pl.pallas_call(
        _kv_kernel,
        out_shape=[jax.ShapeDtypeStruct((bsz, N_HEADS, HEAD_TILE, seq), BF16),
                   jax.ShapeDtypeStruct((bsz, N_HEADS, seq, HEAD_TILE), BF16),
                   jax.ShapeDtypeStruct((n, HEAD_TILE), F32),
                   jax.ShapeDtypeStruct((n, HEAD_TILE), F32)],
        grid=(n // tm,),
        in_specs=[row(d), row(1), _const_spec((1, HEAD_TILE)), _const_spec((1, d)),
                  _const_spec(wa.shape), _const_spec((1, KV_LORA)), _const_spec(wk.shape),
                  _const_spec(wv.shape), _const_spec((HEAD_TILE, HEAD_TILE)),
                  _const_spec((HEAD_TILE, HEAD_TILE)),
                  _const_spec((1, HEAD_TILE)), _const_spec((HEAD_TILE, 1))],
        out_specs=[kt_spec, _head_major_spec(N_HEADS, tm, seq), row(HEAD_TILE), row(HEAD_TILE)],
        compiler_params=_params(("parallel",)),
        name="kv_proj",
    )(x2, pos2, _head_tile_freq(inv_freq), gain.reshape(1, d), wa, kv_a_norm.reshape(1, KV_LORA),
      wk, wv, _segment_mean_matrix(), _rope_swap_matrix(), gain_tile,
      gain_tile.reshape(HEAD_TILE, 1))


def _q_kernel(x_ref, cos_ref, sin_ref, g_ref, wa_ref, an_ref, wb_ref, e_ref, sw_ref, gt_ref, q_ref, *,
              scale):
    h = _rms(x_ref[...], g_ref[...]).astype(BF16)
    c_q = _rms(jnp.dot(h, wa_ref[...], preferred_element_type=F32), an_ref[...]).astype(BF16)
    q = jnp.dot(c_q, wb_ref[...], preferred_element_type=F32)
    cos2 = jnp.concatenate([cos_ref[...] * scale] * 2, axis=-1)
    sin2 = jnp.concatenate([sin_ref[...] * scale] * 2, axis=-1)
    gt2 = jnp.concatenate([gt_ref[...]] * 2, axis=-1)
    e = e_ref[...]
    swap = sw_ref[...]
    for pair in range(N_HEADS // 2):
        t2 = q[:, 2 * pair * HEAD_TILE:(2 * pair + 2) * HEAD_TILE]
        ms2 = jnp.dot((t2 * t2).astype(BF16), e, preferred_element_type=F32)
        out = _rotate(t2 * lax.rsqrt(ms2 + EPS) * gt2, cos2, sin2, swap).astype(BF16)
        q_ref[2 * pair] = out[:, :HEAD_TILE]
        q_ref[2 * pair + 1] = out[:, HEAD_TILE:]


def _q_proj(x2, cos_t, sin_t, gain, w_q_a, q_a_norm, w_q_b, q_nope_norm, q_rope_norm, *, tm, bsz):
    n, d = x2.shape
    seq = n // bsz
    rank = w_q_a.shape[1]
    wb = w_q_b.reshape(rank, N_HEADS, QK_DIM)
    wb = jnp.pad(wb, ((0, 0), (0, 0), (0, HEAD_TILE - QK_DIM)))
    wb = wb.reshape(rank, N_HEADS * HEAD_TILE).astype(BF16)
    row = lambda w: pl.BlockSpec((tm, w), lambda i: (i, 0))
    return pl.pallas_call(
        functools.partial(_q_kernel, scale=math.log2(math.e) / math.sqrt(QK_DIM)),
        out_shape=jax.ShapeDtypeStruct((bsz, N_HEADS, seq, HEAD_TILE), BF16),
        grid=(n // tm,),
        in_specs=[row(d), row(HEAD_TILE), row(HEAD_TILE), _const_spec((1, d)),
                  _const_spec((d, rank)), _const_spec((1, rank)), _const_spec(wb.shape),
                  _const_spec((2 * HEAD_TILE, 2 * HEAD_TILE)),
                  _const_spec((2 * HEAD_TILE, 2 * HEAD_TILE)), _const_spec((1, HEAD_TILE))],
        out_specs=_head_major_spec(N_HEADS, tm, seq),
        compiler_params=_params(("parallel",)),
        name="q_proj",
    )(x2, cos_t, sin_t, gain.reshape(1, d), w_q_a.astype(BF16),
      q_a_norm.reshape(1, rank), wb, _segment_mean_matrix(2), _rope_swap_matrix(2),
      _head_tile_gain(q_nope_norm, q_rope_norm))


MAX_HEADS_IN_FLIGHT = 8
SCORE_SCRATCH_SEQS = 4


def _attn_kernel(q_ref, kt_ref, v_ref, o_ref, s_ref, m_ref, p_ref, *, tq, nq):
    i = pl.program_id(1)
    nh = q_ref.shape[0]
    row = lax.broadcasted_iota(jnp.int32, (tq, tq), 0)
    col = lax.broadcasted_iota(jnp.int32, (tq, tq), 1)
    causal = col <= row
    lane_tiles = tq // LANES

    def stage_a(h, base, ms, nk):
        qh = q_ref[h]
        mx = None
        for j in range(nk):
            s = jnp.dot(qh, kt_ref[h, :, j * tq:(j + 1) * tq], preferred_element_type=F32)
            if j == nk - 1:
                s = jnp.where(causal, s, NEG_INF)
            s_ref[:, base + j * tq:base + (j + 1) * tq] = s
            for t in range(lane_tiles):
                f = s[:, t * LANES:(t + 1) * LANES]
                mx = f if mx is None else jnp.maximum(mx, f)
        m = jnp.max(mx, axis=-1, keepdims=True)
        m_ref[ms] = jnp.broadcast_to(m, (tq, LANES))

    def stage_b(h, base, ms, nk):
        m = m_ref[ms]
        m = jnp.concatenate([m] * lane_tiles, axis=-1)
        for j in range(nk):
            cols = slice(base + j * tq, base + (j + 1) * tq)
            p_ref[:, cols] = jnp.exp2((s_ref[:, cols] - m).astype(BF16))
        acc = jnp.dot(p_ref[:, base:base + nk * tq], v_ref[h, :nk * tq, :],
                      preferred_element_type=F32)
        odd = (h % 2) == 1
        denom = jnp.where(odd, acc[:, _ones_lane(1):_ones_lane(1) + 1],
                          acc[:, _ones_lane(0):_ones_lane(0) + 1])
        o_ref[h] = (acc / denom).astype(BF16)

    for c in range(nq):
        @pl.when(i == c)
        def _(c=c):
            nk = c + 1
            hpi = min(m_ref.shape[0], s_ref.shape[1] // (nk * tq))
            hpi = 1 << (hpi.bit_length() - 1)

            def body(it, carry):
                for k in range(hpi):
                    stage_a(it * hpi + k, k * nk * tq, k, nk)
                for k in range(hpi):
                    stage_b(it * hpi + k, k * nk * tq, k, nk)
                return carry
            lax.fori_loop(0, nh // hpi, body, 0)


def _attention(q, kt, v, *, tq):
    bsz, nh, seq, _ = q.shape
    nq = seq // tq
    head_tile = lambda rows: pl.BlockSpec((None, nh, rows, HEAD_TILE), lambda b, i: (b, 0, i, 0))
    whole = lambda shape: pl.BlockSpec((None,) + shape, lambda b, i: (b, 0, 0, 0))
    return pl.pallas_call(
        functools.partial(_attn_kernel, tq=tq, nq=nq),
        out_shape=jax.ShapeDtypeStruct((bsz, nh, seq, HEAD_TILE), BF16),
        grid=(bsz, nq),
        in_specs=[head_tile(tq), whole((nh, HEAD_TILE, seq)), whole((nh, seq, HEAD_TILE))],
        out_specs=head_tile(tq),
        scratch_shapes=[pltpu.VMEM((tq, SCORE_SCRATCH_SEQS * seq), F32),
                        pltpu.VMEM((MAX_HEADS_IN_FLIGHT, tq, LANES), F32),
                        pltpu.VMEM((tq, SCORE_SCRATCH_SEQS * seq), BF16)],
        compiler_params=_params(("parallel", "arbitrary")),
        name="causal_attention",
    )(q, kt, v)


def _tiles(bsz, seq):
    tm = 512 if seq % 512 == 0 else seq
    tc = 16 if seq % 16 == 0 else seq
    tq = 256 if seq % 256 == 0 else seq
    return tm, tc, tq


def kernel(x, positions, mix_norm, ffn_norm, ffn_w_gate_up, ffn_w_down, ssm_w_in, ssm_lambda_re, ssm_lambda_im, ssm_log_step, ssm_b_re, ssm_b_im, ssm_c_re, ssm_c_im, ssm_d, ssm_w_glu, kv_in_norm, mla_w_kv_a, mla_kv_a_norm, mla_w_kv_b, mla_k_nope_norm, mla_k_rope_norm, mla_w_q_a, mla_q_a_norm, mla_w_q_b, mla_q_nope_norm, mla_q_rope_norm, mla_w_o):
    bsz, seq, d = x.shape
    n = bsz * seq
    depth = mix_norm.shape[0]
    n_ssm = ssm_w_in.shape[0]
    assert bsz % 8 == 0 and d % CH_TILE == 0
    tm, tc, tq = _tiles(bsz, seq)

    xs = x
    for i in range(n_ssm):
        a_re, a_im, br, bi = _discretise(ssm_lambda_re[i], ssm_lambda_im[i], ssm_log_step[i],
                                         ssm_b_re[i], ssm_b_im[i])
        b_blk, c_blk = _block_diag_weights(br, bi, ssm_c_re[i], ssm_c_im[i])
        xs = _s5_layer(xs, mix_norm[i], ssm_w_in[i], b_blk, a_re, a_im, c_blk, ssm_d[i],
                       ssm_w_glu[i], tc=tc, batch_major_in=(i == 0))
        xs = _ffn_layer(xs.reshape(n, d), ffn_norm[i], ffn_w_gate_up[i], ffn_w_down[i], tm=tm,
                        batch_major_out=bsz if i == n_ssm - 1 else None)
        if i < n_ssm - 1:
            xs = xs.reshape(seq, bsz, d)
    x2 = xs.reshape(n, d)

    pos2 = positions.reshape(n, 1)
    inv_freq = ROPE_THETA ** (-jnp.arange(0, QK_ROPE, 2, dtype=F32) / QK_ROPE)
    k, v, cos_t, sin_t = _kv_proj(x2, pos2, inv_freq, kv_in_norm, mla_w_kv_a, mla_kv_a_norm,
                                  mla_w_kv_b, mla_k_nope_norm, mla_k_rope_norm, tm=tm, bsz=bsz)
    for layer in range(n_ssm, depth):
        j = layer - n_ssm
        q = _q_proj(x2, cos_t, sin_t, mix_norm[layer], mla_w_q_a[j], mla_q_a_norm[j],
                    mla_w_q_b[j], mla_q_nope_norm[j], mla_q_rope_norm[j], tm=tm, bsz=bsz)
        o = _attention(q, k, v, tq=tq)
        x2 = _ffn_layer(x2, ffn_norm[layer], ffn_w_gate_up[layer], ffn_w_down[layer], tm=tm,
                        attn=o, w_o=mla_w_o[j])
    return x2.reshape(bsz, seq, d)
```

```python
import functools
import math

import jax
import jax.numpy as jnp
from jax import lax
from jax.experimental import pallas as pl
from jax.experimental.pallas import tpu as pltpu

F32 = jnp.float32
BF16 = jnp.bfloat16

EPS = 1e-6
NEG_INF = -1e30
ROPE_THETA = 10000.0

SSM_GROUP = 16
SSM_STATE = 64
N_HEADS = 16
QK_NOPE = 64
QK_ROPE = 32
HALF_ROPE = QK_ROPE // 2
QK_DIM = QK_NOPE + QK_ROPE
V_DIM = 64
KV_LORA = 256

LANES = 128
MXU_TILE = 256
VMEM_LIMIT = 56 * 1024 * 1024

HEAD_TILE = LANES
CH_TILE = MXU_TILE
GROUPS_PER_TILE = CH_TILE // SSM_GROUP
STATES_PER_TILE = GROUPS_PER_TILE * SSM_STATE
SCAN_LANES = LANES


def _rms(x, gain):
    ms = jnp.mean(x * x, axis=-1, keepdims=True)
    return x * lax.rsqrt(ms + EPS) * gain


def _const_spec(shape):
    nd = len(shape)
    return pl.BlockSpec(shape, lambda *_: (0,) * nd, pipeline_mode=pl.Buffered(1))


def _params(semantics):
    return pltpu.CompilerParams(dimension_semantics=semantics, vmem_limit_bytes=VMEM_LIMIT)


def _disc_kernel(lr_ref, li_ref, ls_ref, bre_ref, bim_ref, are_ref, aim_ref, br_ref, bi_ref):
    lr = lr_ref[...]
    li = li_ref[...]
    step = jnp.exp(ls_ref[...])
    decay = jnp.exp(lr * step)
    a_re = decay * jnp.cos(li * step)
    a_im = decay * jnp.sin(li * step)
    den = lr * lr + li * li
    f_re = ((a_re - 1.0) * lr + a_im * li) / den
    f_im = (a_im * lr - (a_re - 1.0) * li) / den
    b_re = bre_ref[...]
    b_im = bim_ref[...]
    are_ref[...] = a_re
    aim_ref[...] = a_im
    br_ref[...] = f_re * b_re - f_im * b_im
    bi_ref[...] = f_re * b_im + f_im * b_re


def _discretise(lam_re, lam_im, log_step, b_re, b_im):
    g, n = lam_re.shape
    k = b_re.shape[-1]
    gn = g * n
    col = lambda a: a.reshape(gn, 1)
    ls = jnp.broadcast_to(log_step[:, None], (g, n))
    outs = pl.pallas_call(
        _disc_kernel,
        out_shape=[jax.ShapeDtypeStruct((gn, 1), F32), jax.ShapeDtypeStruct((gn, 1), F32),
                   jax.ShapeDtypeStruct((gn, k), F32), jax.ShapeDtypeStruct((gn, k), F32)],
        name="s5_discretise",
    )(col(lam_re), col(lam_im), col(ls), b_re.reshape(gn, k), b_im.reshape(gn, k))
    a_re, a_im, br, bi = outs
    return a_re.reshape(g, n), a_im.reshape(g, n), br.reshape(g, n, k), bi.reshape(g, n, k)


def _block_diag_weights(br, bi, c_re, c_im):
    g, n, k = br.shape
    tiles = g // GROUPS_PER_TILE
    eye = jnp.eye(GROUPS_PER_TILE, dtype=F32)

    def bmat(b):
        b4 = b.reshape(tiles, GROUPS_PER_TILE, n, k)
        return jnp.einsum('qgnk,gh->qgkhn', b4, eye).reshape(tiles, CH_TILE, STATES_PER_TILE)

    def cmat(c):
        c4 = c.reshape(tiles, GROUPS_PER_TILE, k, n)
        return jnp.einsum('qgkn,gh->qgnhk', c4, eye).reshape(tiles, STATES_PER_TILE, CH_TILE)

    chunks = STATES_PER_TILE // SCAN_LANES
    b_blk = jnp.stack([bmat(br).reshape(tiles, CH_TILE, chunks, SCAN_LANES),
                       bmat(bi).reshape(tiles, CH_TILE, chunks, SCAN_LANES)], axis=3)
    c_blk = jnp.stack([cmat(c_re).reshape(tiles, chunks, SCAN_LANES, CH_TILE),
                       -cmat(c_im).reshape(tiles, chunks, SCAN_LANES, CH_TILE)], axis=2)
    return (b_blk.reshape(tiles, CH_TILE, 2 * STATES_PER_TILE).astype(BF16),
            c_blk.reshape(tiles, 2 * STATES_PER_TILE, CH_TILE).astype(BF16))


def _s5_kernel(x_ref, g_ref, win_ref, b_ref, are_ref, aim_ref, c_ref, d_ref, wglu_ref,
               o_ref, st_ref, bu_ref, z_ref, *, tc, nb, d_model, batch_major_in):
    rows = tc * nb
    tiles = d_model // CH_TILE

    @pl.when(pl.program_id(0) == 0)
    def _():
        st_ref[...] = jnp.zeros_like(st_ref)

    x = x_ref[...]
    if batch_major_in:
        x = pltpu.einshape("btd->tbd", x)
    x = x.reshape(rows, d_model)
    h = _rms(x, g_ref[...]).astype(BF16)
    u = jnp.dot(h, win_ref[...], preferred_element_type=F32)
    u_bf = u.astype(BF16)

    for q in range(tiles):
        ch = slice(q * CH_TILE, (q + 1) * CH_TILE)
        y = d_ref[:, ch] * u[:, ch]
        for c in range(STATES_PER_TILE // SCAN_LANES):
            cols = slice(2 * c * SCAN_LANES, 2 * (c + 1) * SCAN_LANES)
            re = slice(2 * c * SCAN_LANES, (2 * c + 1) * SCAN_LANES)
            im = slice((2 * c + 1) * SCAN_LANES, (2 * c + 2) * SCAN_LANES)
            st = slice(c * SCAN_LANES, (c + 1) * SCAN_LANES)
            bu_ref[:, cols] = jnp.dot(u_bf[:, ch], b_ref[q, :, cols], preferred_element_type=F32)
            ar = jnp.broadcast_to(are_ref[q, :, st], (nb, SCAN_LANES))
            ai = jnp.broadcast_to(aim_ref[q, :, st], (nb, SCAN_LANES))
            sr, si = st_ref[q, :, re], st_ref[q, :, im]
            for t in range(tc):
                rws = slice(t * nb, (t + 1) * nb)
                nr = ar * sr - ai * si + bu_ref[rws, re]
                ni = ar * si + ai * sr + bu_ref[rws, im]
                bu_ref[rws, re] = nr
                bu_ref[rws, im] = ni
                sr, si = nr, ni
            st_ref[q, :, re] = sr
            st_ref[q, :, im] = si
            y = y + jnp.dot(bu_ref[:, cols].astype(BF16), c_ref[q, cols, :],
                            preferred_element_type=F32)
        z_ref[:, ch] = jax.nn.gelu(y).astype(BF16)

    glu = jnp.dot(z_ref[...], wglu_ref[...], preferred_element_type=F32)
    out = x + glu[:, :d_model] * jax.nn.sigmoid(glu[:, d_model:])
    o_ref[...] = out.reshape(tc, nb, d_model)


def _s5_layer(xin, gain, w_in, b_blk, a_re, a_im, c_blk, d_skip, w_glu, *, tc, batch_major_in):
    if batch_major_in:
        nb, seq, d = xin.shape
        x_spec = pl.BlockSpec((nb, tc, d), lambda i: (0, i, 0))
    else:
        seq, nb, d = xin.shape
        x_spec = pl.BlockSpec((tc, nb, d), lambda i: (i, 0, 0))
    tiles = d // CH_TILE
    rows = tc * nb
    kern = functools.partial(_s5_kernel, tc=tc, nb=nb, d_model=d, batch_major_in=batch_major_in)
    return pl.pallas_call(
        kern,
        out_shape=jax.ShapeDtypeStruct((seq, nb, d), F32),
        grid=(seq // tc,),
        in_specs=[
            x_spec,
            _const_spec((1, d)),
            _const_spec((d, d)),
            _const_spec((tiles, CH_TILE, 2 * STATES_PER_TILE)),
            _const_spec((tiles, 1, STATES_PER_TILE)),
            _const_spec((tiles, 1, STATES_PER_TILE)),
            _const_spec((tiles, 2 * STATES_PER_TILE, CH_TILE)),
            _const_spec((1, d)),
            _const_spec((d, 2 * d)),
        ],
        out_specs=pl.BlockSpec((tc, nb, d), lambda i: (i, 0, 0)),
        scratch_shapes=[
            pltpu.VMEM((tiles, nb, 2 * STATES_PER_TILE), F32),
            pltpu.VMEM((rows, 2 * STATES_PER_TILE), F32),
            pltpu.VMEM((rows, d), BF16),
        ],
        compiler_params=_params(("arbitrary",)),
        name="s5_layer",
    )(xin, gain.reshape(1, d), w_in.astype(BF16), b_blk,
      a_re.reshape(tiles, 1, STATES_PER_TILE), a_im.reshape(tiles, 1, STATES_PER_TILE),
      c_blk, d_skip.reshape(1, d), w_glu.astype(BF16))


FF_CHUNK = MXU_TILE


def _ffn_body(x, g_ref, wgu_ref, wd_ref, d_ff):
    h = _rms(x, g_ref[...]).astype(BF16)
    acc = jnp.zeros(x.shape, F32)
    for c in range(d_ff // FF_CHUNK):
        gate = jnp.dot(h, wgu_ref[:, c * FF_CHUNK:(c + 1) * FF_CHUNK], preferred_element_type=F32)
        up = jnp.dot(h, wgu_ref[:, d_ff + c * FF_CHUNK:d_ff + (c + 1) * FF_CHUNK],
                     preferred_element_type=F32)
        act = (jax.nn.silu(gate) * up).astype(BF16)
        acc = acc + jnp.dot(act, wd_ref[c * FF_CHUNK:(c + 1) * FF_CHUNK, :],
                            preferred_element_type=F32)
    return x + acc


def _ffn_kernel(x_ref, g_ref, wgu_ref, wd_ref, o_ref, *, d_ff):
    out = _ffn_body(x_ref[...], g_ref, wgu_ref, wd_ref, d_ff)
    if len(o_ref.shape) == 3:
        nb, steps, d = o_ref.shape
        out = pltpu.einshape("tbd->btd", out.reshape(steps, nb, d))
    o_ref[...] = out


def _proj_ffn_kernel(x_ref, a_ref, wo_ref, g_ref, wgu_ref, wd_ref, o_ref, *, d_ff):
    lane = lax.broadcasted_iota(jnp.int32, a_ref.shape[1:], 1)
    attn = jnp.concatenate([jnp.where(lane < V_DIM, a_ref[2 * p], a_ref[2 * p + 1])
                            for p in range(a_ref.shape[0] // 2)], axis=-1)
    x = x_ref[...] + jnp.dot(attn, wo_ref[...], preferred_element_type=F32)
    o_ref[...] = _ffn_body(x, g_ref, wgu_ref, wd_ref, d_ff)


def _ffn_layer(x2, gain, w_gate_up, w_down, *, tm, attn=None, w_o=None, batch_major_out=None):
    n, d = x2.shape
    d_ff = w_down.shape[0]
    row_spec = pl.BlockSpec((tm, d), lambda i: (i, 0))
    out_shape, out_spec = (n, d), row_spec
    if batch_major_out is not None:
        nb = batch_major_out
        out_shape = (nb, n // nb, d)
        out_spec = pl.BlockSpec((nb, tm // nb, d), lambda i: (0, i, 0))
    w_specs = [_const_spec((1, d)), _const_spec((d, 2 * d_ff)), _const_spec((d_ff, d))]
    w_args = (gain.reshape(1, d), w_gate_up.astype(BF16), w_down.astype(BF16))
    if attn is None:
        kern, in_specs, args = _ffn_kernel, [row_spec] + w_specs, (x2,) + w_args
    else:
        _, heads, seq, width = attn.shape
        per_b = seq // tm
        kern = _proj_ffn_kernel
        attn_spec = pl.BlockSpec((None, heads, tm, width), lambda i: (i // per_b, 0, i % per_b, 0))
        in_specs = [row_spec, attn_spec, _const_spec((heads // 2 * width, d))] + w_specs
        args = (x2, attn, w_o.astype(BF16)) + w_args
    return pl.pallas_call(
        functools.partial(kern, d_ff=d_ff),
        out_shape=jax.ShapeDtypeStruct(out_shape, F32),
        grid=(n // tm,),
        in_specs=in_specs,
        out_specs=out_spec,
        compiler_params=_params(("parallel",)),
        name="ffn_layer",
    )(*args)


def _rope_tables(pos_ref, freq_ref):
    ang = pos_ref[...].astype(F32) * freq_ref[...]
    return jnp.cos(ang), jnp.sin(ang)


def _rotate(tn, cos_t, sin_t, swap):
    partner = jnp.dot(tn.astype(BF16), swap, preferred_element_type=F32)
    return tn * cos_t + partner * sin_t


def _rope_swap_matrix(tiles=1):
    src = jnp.arange(HEAD_TILE)[:, None]
    dst = jnp.arange(HEAD_TILE)[None, :]
    x1_col = (dst >= QK_NOPE) & (dst < QK_NOPE + HALF_ROPE)
    x2_col = (dst >= QK_NOPE + HALF_ROPE) & (dst < QK_DIM)
    p = (jnp.where(x1_col & (src == dst + HALF_ROPE), -1.0, 0.0)
         + jnp.where(x2_col & (src == dst - HALF_ROPE), 1.0, 0.0))
    return jnp.kron(jnp.eye(tiles, dtype=F32), p).astype(BF16)


def _segment_mean_matrix(tiles=1):
    idx = jnp.arange(HEAD_TILE)
    seg = jnp.where(idx < QK_NOPE, 0, jnp.where(idx < QK_DIM, 1, 2))
    same = (seg[:, None] == seg[None, :]) & (seg[:, None] < 2)
    width = jnp.where(seg == 0, QK_NOPE, QK_ROPE).astype(F32)
    e = jnp.where(same, 1.0 / width[None, :], 0.0)
    return jnp.kron(jnp.eye(tiles, dtype=F32), e).astype(BF16)


def _head_tile_freq(inv_freq):
    z = jnp.zeros((QK_NOPE,), F32)
    pad = jnp.zeros((HEAD_TILE - QK_DIM,), F32)
    return jnp.concatenate([z, inv_freq, inv_freq, pad]).reshape(1, HEAD_TILE)


def _head_tile_gain(nope_gain, rope_gain):
    pad = jnp.zeros((HEAD_TILE - QK_DIM,), F32)
    return jnp.concatenate([nope_gain, rope_gain, pad]).reshape(1, HEAD_TILE)


def _kv_kernel(x_ref, pos_ref, freq_ref, g_ref, wa_ref, an_ref, wk_ref, wv_ref, e_ref, sw_ref,
               gt_ref, gc_ref, k_ref, v_ref, cos_ref, sin_ref):
    h = _rms(x_ref[...], g_ref[...]).astype(BF16)
    ckv = jnp.dot(h, wa_ref[...], preferred_element_type=F32)
    c_kv = _rms(ckv[:, :KV_LORA], an_ref[...]).astype(BF16)
    cos_t, sin_t = _rope_tables(pos_ref, freq_ref)
    cos_ref[...] = cos_t
    sin_ref[...] = sin_t
    gt = gt_ref[...]
    e = e_ref[...]

    kr = ckv[:, KV_LORA:]
    ms = jnp.dot((kr * kr).astype(BF16), e, preferred_element_type=F32)
    kr = _rotate(kr * lax.rsqrt(ms + EPS) * gt, cos_t, sin_t, sw_ref[...])
    lane = lax.broadcasted_iota(jnp.int32, kr.shape, 1)
    tm = kr.shape[0]

    kr_t = kr.T
    kn_t = lax.dot_general(wk_ref[...], c_kv, (((1,), (1,)), ((), ())),
                           preferred_element_type=F32)
    rowid = lax.broadcasted_iota(jnp.int32, (HEAD_TILE, tm), 0)
    gcol = jnp.broadcast_to(gc_ref[...], (HEAD_TILE, tm))
    for hd in range(N_HEADS):
        t = kn_t[hd * HEAD_TILE:(hd + 1) * HEAD_TILE, :]
        nope = t[:QK_NOPE]
        ms = jnp.sum(nope * nope, axis=0, keepdims=True) * (1.0 / QK_NOPE)
        tn = t * lax.rsqrt(ms + EPS) * gcol
        k_ref[hd] = jnp.where(rowid < QK_NOPE, tn, kr_t).astype(BF16)
    v = jnp.dot(c_kv, wv_ref[...], preferred_element_type=F32)
    for hd in range(N_HEADS):
        tile = v[:, hd * HEAD_TILE:(hd + 1) * HEAD_TILE]
        v_ref[hd] = jnp.where(lane == _ones_lane(hd % 2), 1.0, tile).astype(BF16)


def _ones_lane(parity):
    return V_DIM if parity == 0 else HEAD_TILE - V_DIM - 1


def _head_major_spec(heads, tm, seq):
    per_b = seq // tm
    return pl.BlockSpec((None, heads, tm, HEAD_TILE), lambda i: (i // per_b, 0, i % per_b, 0))


def _kv_proj(x2, pos2, inv_freq, gain, w_kv_a, kv_a_norm, w_kv_b, k_nope_norm, k_rope_norm, *,
             tm, bsz):
    n, d = x2.shape
    seq = n // bsz
    wa_rope = jnp.pad(w_kv_a[:, KV_LORA:], ((0, 0), (QK_NOPE, HEAD_TILE - QK_DIM)))
    wa = jnp.concatenate([w_kv_a[:, :KV_LORA], wa_rope], axis=1).astype(BF16)
    wb = w_kv_b.reshape(KV_LORA, N_HEADS, QK_NOPE + V_DIM)
    wk = jnp.pad(wb[:, :, :QK_NOPE], ((0, 0), (0, 0), (0, HEAD_TILE - QK_NOPE)))
    wk = wk.reshape(KV_LORA, N_HEADS * HEAD_TILE).T.astype(BF16)
    wv = wb[:, :, QK_NOPE:]
    odd_head = (jnp.arange(N_HEADS) % 2 == 1)[None, :, None]
    wv = jnp.where(odd_head, jnp.pad(wv, ((0, 0), (0, 0), (HEAD_TILE - V_DIM, 0))),
                   jnp.pad(wv, ((0, 0), (0, 0), (0, HEAD_TILE - V_DIM))))
    wv = wv.reshape(KV_LORA, N_HEADS * HEAD_TILE).astype(BF16)
    gain_tile = _head_tile_gain(k_nope_norm, k_rope_norm)
    row = lambda w: pl.BlockSpec((tm, w), lambda i: (i, 0))
    per_b = seq // tm
    kt_spec = pl.BlockSpec((None, N_HEADS, HEAD_TILE, tm), lambda i: (i // per_b, 0, 0, i % per_b))
    return pl.pallas_call(
        _kv_kernel,
        out_shape=[jax.ShapeDtypeStruct((bsz, N_HEADS, HEAD_TILE, seq), BF16),
                   jax.ShapeDtypeStruct((bsz, N_HEADS, seq, HEAD_TILE), BF16),
                   jax.ShapeDtypeStruct((n, HEAD_TILE), F32),
                   jax.ShapeDtypeStruct((n, HEAD_TILE), F32)],
        grid=(n // tm,),
        in_specs=[row(d), row(1), _const_spec((1, HEAD_TILE)), _const_spec((1, d)),
                  _const_spec(wa.shape), _const_spec((1, KV_LORA)), _const_spec(wk.shape),
                  _const_spec(wv.shape), _const_spec((HEAD_TILE, HEAD_TILE)),
                  _const_spec((HEAD_TILE, HEAD_TILE)),
                  _const_spec((1, HEAD_TILE)), _const_spec((HEAD_TILE, 1))],
        out_specs=[kt_spec, _head_major_spec(N_HEADS, tm, seq), row(HEAD_TILE), row(HEAD_TILE)],
        compiler_params=_params(("parallel",)),
        name="kv_proj",
    )(x2, pos2, _head_tile_freq(inv_freq), gain.reshape(1, d), wa, kv_a_norm.reshape(1, KV_LORA),
      wk, wv, _segment_mean_matrix(), _rope_swap_matrix(), gain_tile,
      gain_tile.reshape(HEAD_TILE, 1))


def _q_kernel(x_ref, cos_ref, sin_ref, g_ref, wa_ref, an_ref, wb_ref, e_ref, sw_ref, gt_ref, q_ref, *,
              scale):
    h = _rms(x_ref[...], g_ref[...]).astype(BF16)
    c_q = _rms(jnp.dot(h, wa_ref[...], preferred_element_type=F32), an_ref[...]).astype(BF16)
    cos2 = jnp.concatenate([cos_ref[...] * scale] * 2, axis=-1)
    sin2 = jnp.concatenate([sin_ref[...] * scale] * 2, axis=-1)
    gt2 = jnp.concatenate([gt_ref[...]] * 2, axis=-1)
    e = e_ref[...]
    swap = sw_ref[...]
    for pair in range(N_HEADS // 2):
        t2 = jnp.dot(c_q, wb_ref[:, 2 * pair * HEAD_TILE:(2 * pair + 2) * HEAD_TILE],
                     preferred_element_type=F32)
        ms2 = jnp.dot((t2 * t2).astype(BF16), e, preferred_element_type=F32)
        out = _rotate(t2 * lax.rsqrt(ms2 + EPS) * gt2, cos2, sin2, swap).astype(BF16)
        q_ref[2 * pair] = out[:, :HEAD_TILE]
        q_ref[2 * pair + 1] = out[:, HEAD_TILE:]


def _q_proj(x2, cos_t, sin_t, gain, w_q_a, q_a_norm, w_q_b, q_nope_norm, q_rope_norm, *, tm, bsz):
    n, d = x2.shape
    seq = n // bsz
    rank = w_q_a.shape[1]
    wb = w_q_b.reshape(rank, N_HEADS, QK_DIM)
    wb = jnp.pad(wb, ((0, 0), (0, 0), (0, HEAD_TILE - QK_DIM)))
    wb = wb.reshape(rank, N_HEADS * HEAD_TILE).astype(BF16)
    row = lambda w: pl.BlockSpec((tm, w), lambda i: (i, 0))
    return pl.pallas_call(
        functools.partial(_q_kernel, scale=math.log2(math.e) / math.sqrt(QK_DIM)),
        out_shape=jax.ShapeDtypeStruct((bsz, N_HEADS, seq, HEAD_TILE), BF16),
        grid=(n // tm,),
        in_specs=[row(d), row(HEAD_TILE), row(HEAD_TILE), _const_spec((1, d)),
                  _const_spec((d, rank)), _const_spec((1, rank)), _const_spec(wb.shape),
                  _const_spec((2 * HEAD_TILE, 2 * HEAD_TILE)),
                  _const_spec((2 * HEAD_TILE, 2 * HEAD_TILE)), _const_spec((1, HEAD_TILE))],
        out_specs=_head_major_spec(N_HEADS, tm, seq),
        compiler_params=_params(("parallel",)),
        name="q_proj",
    )(x2, cos_t, sin_t, gain.reshape(1, d), w_q_a.astype(BF16),
      q_a_norm.reshape(1, rank), wb, _segment_mean_matrix(2), _rope_swap_matrix(2),
      _head_tile_gain(q_nope_norm, q_rope_norm))


MAX_HEADS_IN_FLIGHT = 8
SCORE_SCRATCH_SEQS = 6


def _attn_kernel(q_ref, kt_ref, v_ref, o_ref, s_ref, m_ref, *, tq, nq):
    i = pl.program_id(1)
    nh = q_ref.shape[0]
    row = lax.broadcasted_iota(jnp.int32, (tq, tq), 0)
    col = lax.broadcasted_iota(jnp.int32, (tq, tq), 1)
    causal = col <= row
    lane_tiles = tq // LANES

    def stage_a(h, base, ms, nk):
        qh = q_ref[h]
        mx = None
        for j in range(nk):
            s = jnp.dot(qh, kt_ref[h, :, j * tq:(j + 1) * tq], preferred_element_type=F32)
            if j == nk - 1:
                s = jnp.where(causal, s, NEG_INF)
            s_ref[:, base + j * tq:base + (j + 1) * tq] = s
            for t in range(lane_tiles):
                f = s[:, t * LANES:(t + 1) * LANES]
                mx = f if mx is None else jnp.maximum(mx, f)
        m = jnp.max(mx, axis=-1, keepdims=True)
        m_ref[ms] = jnp.broadcast_to(m, (tq, LANES))

    def stage_b(h, base, ms, nk):
        m = m_ref[ms]
        m = jnp.concatenate([m] * lane_tiles, axis=-1)
        acc = None
        for j in range(nk):
            cols = slice(base + j * tq, base + (j + 1) * tq)
            p = jnp.exp2((s_ref[:, cols] - m).astype(BF16))
            pv = jnp.dot(p, v_ref[h, j * tq:(j + 1) * tq, :], preferred_element_type=F32)
            acc = pv if acc is None else acc + pv
        odd = (h % 2) == 1
        denom = jnp.where(odd, acc[:, _ones_lane(1):_ones_lane(1) + 1],
                          acc[:, _ones_lane(0):_ones_lane(0) + 1])
        o_ref[h] = (acc / denom).astype(BF16)

    for c in range(nq):
        @pl.when(i == c)
        def _(c=c):
            nk = c + 1
            hpi = min(m_ref.shape[0], s_ref.shape[1] // (nk * tq))
            hpi = 1 << (hpi.bit_length() - 1)

            def body(it, carry):
                for k in range(hpi):
                    stage_a(it * hpi + k, k * nk * tq, k, nk)
                for k in range(hpi):
                    stage_b(it * hpi + k, k * nk * tq, k, nk)
                return carry
            lax.fori_loop(0, nh // hpi, body, 0)


def _attention(q, kt, v, *, tq):
    bsz, nh, seq, _ = q.shape
    nq = seq // tq
    head_tile = lambda rows: pl.BlockSpec((None, nh, rows, HEAD_TILE), lambda b, i: (b, 0, i, 0))
    whole = lambda shape: pl.BlockSpec((None,) + shape, lambda b, i: (b, 0, 0, 0))
    return pl.pallas_call(
        functools.partial(_attn_kernel, tq=tq, nq=nq),
        out_shape=jax.ShapeDtypeStruct((bsz, nh, seq, HEAD_TILE), BF16),
        grid=(bsz, nq),
        in_specs=[head_tile(tq), whole((nh, HEAD_TILE, seq)), whole((nh, seq, HEAD_TILE))],
        out_specs=head_tile(tq),
        scratch_shapes=[pltpu.VMEM((tq, SCORE_SCRATCH_SEQS * seq), F32),
                        pltpu.VMEM((MAX_HEADS_IN_FLIGHT, tq, LANES), F32)],
        compiler_params=_params(("parallel", "arbitrary")),
        name="causal_attention",
    )(q, kt, v)


def _tiles(bsz, seq):
    tm = 512 if seq % 512 == 0 else seq
    tc = 16 if seq % 16 == 0 else seq
    tq = 256 if seq % 256 == 0 else seq
    return tm, tc, tq


def kernel(x, positions, mix_norm, ffn_norm, ffn_w_gate_up, ffn_w_down, ssm_w_in, ssm_lambda_re, ssm_lambda_im, ssm_log_step, ssm_b_re, ssm_b_im, ssm_c_re, ssm_c_im, ssm_d, ssm_w_glu, kv_in_norm, mla_w_kv_a, mla_kv_a_norm, mla_w_kv_b, mla_k_nope_norm, mla_k_rope_norm, mla_w_q_a, mla_q_a_norm, mla_w_q_b, mla_q_nope_norm, mla_q_rope_norm, mla_w_o):
    bsz, seq, d = x.shape
    n = bsz * seq
    depth = mix_norm.shape[0]
    n_ssm = ssm_w_in.shape[0]
    assert bsz % 8 == 0 and d % CH_TILE == 0
    tm, tc, tq = _tiles(bsz, seq)

    xs = x
    for i in range(n_ssm):
        a_re, a_im, br, bi = _discretise(ssm_lambda_re[i], ssm_lambda_im[i], ssm_log_step[i],
                                         ssm_b_re[i], ssm_b_im[i])
        b_blk, c_blk = _block_diag_weights(br, bi, ssm_c_re[i], ssm_c_im[i])
        xs = _s5_layer(xs, mix_norm[i], ssm_w_in[i], b_blk, a_re, a_im, c_blk, ssm_d[i],
                       ssm_w_glu[i], tc=tc, batch_major_in=(i == 0))
        xs = _ffn_layer(xs.reshape(n, d), ffn_norm[i], ffn_w_gate_up[i], ffn_w_down[i], tm=tm,
                        batch_major_out=bsz if i == n_ssm - 1 else None)
        if i < n_ssm - 1:
            xs = xs.reshape(seq, bsz, d)
    x2 = xs.reshape(n, d)

    pos2 = positions.reshape(n, 1)
    inv_freq = ROPE_THETA ** (-jnp.arange(0, QK_ROPE, 2, dtype=F32) / QK_ROPE)
    k, v, cos_t, sin_t = _kv_proj(x2, pos2, inv_freq, kv_in_norm, mla_w_kv_a, mla_kv_a_norm,
                                  mla_w_kv_b, mla_k_nope_norm, mla_k_rope_norm, tm=tm, bsz=bsz)
    for layer in range(n_ssm, depth):
        j = layer - n_ssm
        q = _q_proj(x2, cos_t, sin_t, mix_norm[layer], mla_w_q_a[j], mla_q_a_norm[j],
                    mla_w_q_b[j], mla_q_nope_norm[j], mla_q_rope_norm[j], tm=tm, bsz=bsz)
        o = _attention(q, k, v, tq=tq)
        x2 = _ffn_layer(x2, ffn_norm[layer], ffn_w_gate_up[layer], ffn_w_down[layer], tm=tm,
                        attn=o, w_o=mla_w_o[j])
    return x2.reshape(bsz, seq, d)
```

```python
import functools
import math

import jax
import jax.numpy as jnp
from jax import lax
from jax.experimental import pallas as pl
from jax.experimental.pallas import tpu as pltpu

F32 = jnp.float32
BF16 = jnp.bfloat16

EPS = 1e-6
NEG_INF = -1e30
ROPE_THETA = 10000.0

SSM_GROUP = 16
SSM_STATE = 64
N_HEADS = 16
QK_NOPE = 64
QK_ROPE = 32
HALF_ROPE = QK_ROPE // 2
QK_DIM = QK_NOPE + QK_ROPE
V_DIM = 64
KV_LORA = 256

LANES = 128
MXU_TILE = 256
VMEM_LIMIT = 56 * 1024 * 1024

HEAD_TILE = LANES
CH_TILE = MXU_TILE
GROUPS_PER_TILE = CH_TILE // SSM_GROUP
STATES_PER_TILE = GROUPS_PER_TILE * SSM_STATE
SCAN_LANES = LANES


def _rms(x, gain):
    ms = jnp.mean(x * x, axis=-1, keepdims=True)
    return x * lax.rsqrt(ms + EPS) * gain


def _const_spec(shape):
    nd = len(shape)
    return pl.BlockSpec(shape, lambda *_: (0,) * nd, pipeline_mode=pl.Buffered(1))


def _params(semantics):
    return pltpu.CompilerParams(dimension_semantics=semantics, vmem_limit_bytes=VMEM_LIMIT)


def _disc_kernel(lr_ref, li_ref, ls_ref, bre_ref, bim_ref, are_ref, aim_ref, br_ref, bi_ref):
    lr = lr_ref[...]
    li = li_ref[...]
    step = jnp.exp(ls_ref[...])
    decay = jnp.exp(lr * step)
    a_re = decay * jnp.cos(li * step)
    a_im = decay * jnp.sin(li * step)
    den = lr * lr + li * li
    f_re = ((a_re - 1.0) * lr + a_im * li) / den
    f_im = (a_im * lr - (a_re - 1.0) * li) / den
    b_re = bre_ref[...]
    b_im = bim_ref[...]
    are_ref[...] = a_re
    aim_ref[...] = a_im
    br_ref[...] = f_re * b_re - f_im * b_im
    bi_ref[...] = f_re * b_im + f_im * b_re


def _discretise(lam_re, lam_im, log_step, b_re, b_im):
    g, n = lam_re.shape
    k = b_re.shape[-1]
    gn = g * n
    col = lambda a: a.reshape(gn, 1)
    ls = jnp.broadcast_to(log_step[:, None], (g, n))
    outs = pl.pallas_call(
        _disc_kernel,
        out_shape=[jax.ShapeDtypeStruct((gn, 1), F32), jax.ShapeDtypeStruct((gn, 1), F32),
                   jax.ShapeDtypeStruct((gn, k), F32), jax.ShapeDtypeStruct((gn, k), F32)],
        name="s5_discretise",
    )(col(lam_re), col(lam_im), col(ls), b_re.reshape(gn, k), b_im.reshape(gn, k))
    a_re, a_im, br, bi = outs
    return a_re.reshape(g, n), a_im.reshape(g, n), br.reshape(g, n, k), bi.reshape(g, n, k)


def _block_diag_weights(br, bi, c_re, c_im):
    g, n, k = br.shape
    tiles = g // GROUPS_PER_TILE
    eye = jnp.eye(GROUPS_PER_TILE, dtype=F32)

    def bmat(b):
        b4 = b.reshape(tiles, GROUPS_PER_TILE, n, k)
        return jnp.einsum('qgnk,gh->qgkhn', b4, eye).reshape(tiles, CH_TILE, STATES_PER_TILE)

    def cmat(c):
        c4 = c.reshape(tiles, GROUPS_PER_TILE, k, n)
        return jnp.einsum('qgkn,gh->qgnhk', c4, eye).reshape(tiles, STATES_PER_TILE, CH_TILE)

    chunks = STATES_PER_TILE // SCAN_LANES
    b_blk = jnp.stack([bmat(br).reshape(tiles, CH_TILE, chunks, SCAN_LANES),
                       bmat(bi).reshape(tiles, CH_TILE, chunks, SCAN_LANES)], axis=3)
    c_blk = jnp.stack([cmat(c_re).reshape(tiles, chunks, SCAN_LANES, CH_TILE),
                       -cmat(c_im).reshape(tiles, chunks, SCAN_LANES, CH_TILE)], axis=2)
    return (b_blk.reshape(tiles, CH_TILE, 2 * STATES_PER_TILE).astype(BF16),
            c_blk.reshape(tiles, 2 * STATES_PER_TILE, CH_TILE).astype(BF16))


def _s5_kernel(x_ref, g_ref, win_ref, b_ref, are_ref, aim_ref, c_ref, d_ref, wglu_ref,
               o_ref, st_ref, bu_ref, z_ref, *, tc, nb, d_model, batch_major_in):
    rows = tc * nb
    tiles = d_model // CH_TILE

    @pl.when(pl.program_id(0) == 0)
    def _():
        st_ref[...] = jnp.zeros_like(st_ref)

    x = x_ref[...]
    if batch_major_in:
        x = pltpu.einshape("btd->tbd", x)
    x = x.reshape(rows, d_model)
    h = _rms(x, g_ref[...]).astype(BF16)
    u = jnp.dot(h, win_ref[...], preferred_element_type=F32)
    u_bf = u.astype(BF16)

    for q in range(tiles):
        ch = slice(q * CH_TILE, (q + 1) * CH_TILE)
        y = d_ref[:, ch] * u[:, ch]
        for c in range(STATES_PER_TILE // SCAN_LANES):
            cols = slice(2 * c * SCAN_LANES, 2 * (c + 1) * SCAN_LANES)
            re = slice(2 * c * SCAN_LANES, (2 * c + 1) * SCAN_LANES)
            im = slice((2 * c + 1) * SCAN_LANES, (2 * c + 2) * SCAN_LANES)
            st = slice(c * SCAN_LANES, (c + 1) * SCAN_LANES)
            bu_ref[:, cols] = jnp.dot(u_bf[:, ch], b_ref[q, :, cols], preferred_element_type=F32)
            ar = jnp.broadcast_to(are_ref[q, :, st], (nb, SCAN_LANES))
            ai = jnp.broadcast_to(aim_ref[q, :, st], (nb, SCAN_LANES))
            sr, si = st_ref[q, :, re], st_ref[q, :, im]
            for t in range(tc):
                rws = slice(t * nb, (t + 1) * nb)
                nr = ar * sr - ai * si + bu_ref[rws, re]
                ni = ar * si + ai * sr + bu_ref[rws, im]
                bu_ref[rws, re] = nr
                bu_ref[rws, im] = ni
                sr, si = nr, ni
            st_ref[q, :, re] = sr
            st_ref[q, :, im] = si
            y = y + jnp.dot(bu_ref[:, cols].astype(BF16), c_ref[q, cols, :],
                            preferred_element_type=F32)
        z_ref[:, ch] = jax.nn.gelu(y).astype(BF16)

    glu = jnp.dot(z_ref[...], wglu_ref[...], preferred_element_type=F32)
    out = x + glu[:, :d_model] * jax.nn.sigmoid(glu[:, d_model:])
    o_ref[...] = out.reshape(tc, nb, d_model)


def _s5_layer(xin, gain, w_in, b_blk, a_re, a_im, c_blk, d_skip, w_glu, *, tc, batch_major_in):
    if batch_major_in:
        nb, seq, d = xin.shape
        x_spec = pl.BlockSpec((nb, tc, d), lambda i: (0, i, 0))
    else:
        seq, nb, d = xin.shape
        x_spec = pl.BlockSpec((tc, nb, d), lambda i: (i, 0, 0))
    tiles = d // CH_TILE
    rows = tc * nb
    kern = functools.partial(_s5_kernel, tc=tc, nb=nb, d_model=d, batch_major_in=batch_major_in)
    return pl.pallas_call(
        kern,
        out_shape=jax.ShapeDtypeStruct((seq, nb, d), F32),
        grid=(seq // tc,),
        in_specs=[
            x_spec,
            _const_spec((1, d)),
            _const_spec((d, d)),
            _const_spec((tiles, CH_TILE, 2 * STATES_PER_TILE)),
            _const_spec((tiles, 1, STATES_PER_TILE)),
            _const_spec((tiles, 1, STATES_PER_TILE)),
            _const_spec((tiles, 2 * STATES_PER_TILE, CH_TILE)),
            _const_spec((1, d)),
            _const_spec((d, 2 * d)),
        ],
        out_specs=pl.BlockSpec((tc, nb, d), lambda i: (i, 0, 0)),
        scratch_shapes=[
            pltpu.VMEM((tiles, nb, 2 * STATES_PER_TILE), F32),
            pltpu.VMEM((rows, 2 * STATES_PER_TILE), F32),
            pltpu.VMEM((rows, d), BF16),
        ],
        compiler_params=_params(("arbitrary",)),
        name="s5_layer",
    )(xin, gain.reshape(1, d), w_in.astype(BF16), b_blk,
      a_re.reshape(tiles, 1, STATES_PER_TILE), a_im.reshape(tiles, 1, STATES_PER_TILE),
      c_blk, d_skip.reshape(1, d), w_glu.astype(BF16))


FF_CHUNK = MXU_TILE


def _ffn_body(x, g_ref, wgu_ref, wd_ref, d_ff):
    h = _rms(x, g_ref[...]).astype(BF16)
    acc = jnp.zeros(x.shape, F32)
    for c in range(d_ff // FF_CHUNK):
        gate = jnp.dot(h, wgu_ref[:, c * FF_CHUNK:(c + 1) * FF_CHUNK], preferred_element_type=F32)
        up = jnp.dot(h, wgu_ref[:, d_ff + c * FF_CHUNK:d_ff + (c + 1) * FF_CHUNK],
                     preferred_element_type=F32)
        act = (jax.nn.silu(gate) * up).astype(BF16)
        acc = acc + jnp.dot(act, wd_ref[c * FF_CHUNK:(c + 1) * FF_CHUNK, :],
                            preferred_element_type=F32)
    return x + acc


def _ffn_kernel(x_ref, g_ref, wgu_ref, wd_ref, o_ref, *, d_ff):
    out = _ffn_body(x_ref[...], g_ref, wgu_ref, wd_ref, d_ff)
    if len(o_ref.shape) == 3:
        nb, steps, d = o_ref.shape
        out = pltpu.einshape("tbd->btd", out.reshape(steps, nb, d))
    o_ref[...] = out


def _proj_ffn_kernel(x_ref, a_ref, wo_ref, g_ref, wgu_ref, wd_ref, o_ref, *, d_ff):
    lane = lax.broadcasted_iota(jnp.int32, a_ref.shape[1:], 1)
    attn = jnp.concatenate([jnp.where(lane < V_DIM, a_ref[2 * p], a_ref[2 * p + 1])
                            for p in range(a_ref.shape[0] // 2)], axis=-1)
    x = x_ref[...] + jnp.dot(attn, wo_ref[...], preferred_element_type=F32)
    o_ref[...] = _ffn_body(x, g_ref, wgu_ref, wd_ref, d_ff)


def _ffn_layer(x2, gain, w_gate_up, w_down, *, tm, attn=None, w_o=None, batch_major_out=None):
    n, d = x2.shape
    d_ff = w_down.shape[0]
    row_spec = pl.BlockSpec((tm, d), lambda i: (i, 0))
    out_shape, out_spec = (n, d), row_spec
    if batch_major_out is not None:
        nb = batch_major_out
        out_shape = (nb, n // nb, d)
        out_spec = pl.BlockSpec((nb, tm // nb, d), lambda i: (0, i, 0))
    w_specs = [_const_spec((1, d)), _const_spec((d, 2 * d_ff)), _const_spec((d_ff, d))]
    w_args = (gain.reshape(1, d), w_gate_up.astype(BF16), w_down.astype(BF16))
    if attn is None:
        kern, in_specs, args = _ffn_kernel, [row_spec] + w_specs, (x2,) + w_args
    else:
        _, heads, seq, width = attn.shape
        per_b = seq // tm
        kern = _proj_ffn_kernel
        attn_spec = pl.BlockSpec((None, heads, tm, width), lambda i: (i // per_b, 0, i % per_b, 0))
        in_specs = [row_spec, attn_spec, _const_spec((heads // 2 * width, d))] + w_specs
        args = (x2, attn, w_o.astype(BF16)) + w_args
    return pl.pallas_call(
        functools.partial(kern, d_ff=d_ff),
        out_shape=jax.ShapeDtypeStruct(out_shape, F32),
        grid=(n // tm,),
        in_specs=in_specs,
        out_specs=out_spec,
        compiler_params=_params(("parallel",)),
        name="ffn_layer",
    )(*args)


def _rope_tables(pos_ref, freq_ref):
    ang = pos_ref[...].astype(F32) * freq_ref[...]
    return jnp.cos(ang), jnp.sin(ang)


def _rotate(tn, cos_t, sin_t, swap):
    partner = jnp.dot(tn.astype(BF16), swap, preferred_element_type=F32)
    return tn * cos_t + partner * sin_t


def _rope_swap_matrix(tiles=1):
    src = jnp.arange(HEAD_TILE)[:, None]
    dst = jnp.arange(HEAD_TILE)[None, :]
    x1_col = (dst >= QK_NOPE) & (dst < QK_NOPE + HALF_ROPE)
    x2_col = (dst >= QK_NOPE + HALF_ROPE) & (dst < QK_DIM)
    p = (jnp.where(x1_col & (src == dst + HALF_ROPE), -1.0, 0.0)
         + jnp.where(x2_col & (src == dst - HALF_ROPE), 1.0, 0.0))
    return jnp.kron(jnp.eye(tiles, dtype=F32), p).astype(BF16)


def _segment_mean_matrix(tiles=1):
    idx = jnp.arange(HEAD_TILE)
    seg = jnp.where(idx < QK_NOPE, 0, jnp.where(idx < QK_DIM, 1, 2))
    same = (seg[:, None] == seg[None, :]) & (seg[:, None] < 2)
    width = jnp.where(seg == 0, QK_NOPE, QK_ROPE).astype(F32)
    e = jnp.where(same, 1.0 / width[None, :], 0.0)
    return jnp.kron(jnp.eye(tiles, dtype=F32), e).astype(BF16)


def _head_tile_freq(inv_freq):
    z = jnp.zeros((QK_NOPE,), F32)
    pad = jnp.zeros((HEAD_TILE - QK_DIM,), F32)
    return jnp.concatenate([z, inv_freq, inv_freq, pad]).reshape(1, HEAD_TILE)


def _head_tile_gain(nope_gain, rope_gain):
    pad = jnp.zeros((HEAD_TILE - QK_DIM,), F32)
    return jnp.concatenate([nope_gain, rope_gain, pad]).reshape(1, HEAD_TILE)


def _kv_kernel(x_ref, pos_ref, freq_ref, g_ref, wa_ref, an_ref, wk_ref, wv_ref, e_ref, sw_ref,
               gt_ref, gc_ref, k_ref, v_ref, cos_ref, sin_ref):
    h = _rms(x_ref[...], g_ref[...]).astype(BF16)
    ckv = jnp.dot(h, wa_ref[...], preferred_element_type=F32)
    c_kv = _rms(ckv[:, :KV_LORA], an_ref[...]).astype(BF16)
    cos_t, sin_t = _rope_tables(pos_ref, freq_ref)
    cos_ref[...] = cos_t
    sin_ref[...] = sin_t
    gt = gt_ref[...]
    e = e_ref[...]

    kr = ckv[:, KV_LORA:]
    ms = jnp.dot((kr * kr).astype(BF16), e, preferred_element_type=F32)
    kr = _rotate(kr * lax.rsqrt(ms + EPS) * gt, cos_t, sin_t, sw_ref[...])
    lane = lax.broadcasted_iota(jnp.int32, kr.shape, 1)
    tm = kr.shape[0]

    kr_t = kr.T
    kn_t = lax.dot_general(wk_ref[...], c_kv, (((1,), (1,)), ((), ())),
                           preferred_element_type=F32)
    rowid = lax.broadcasted_iota(jnp.int32, (HEAD_TILE, tm), 0)
    gcol = jnp.broadcast_to(gc_ref[...], (HEAD_TILE, tm))
    for hd in range(N_HEADS):
        t = kn_t[hd * HEAD_TILE:(hd + 1) * HEAD_TILE, :]
        nope = t[:QK_NOPE]
        ms = jnp.sum(nope * nope, axis=0, keepdims=True) * (1.0 / QK_NOPE)
        tn = t * lax.rsqrt(ms + EPS) * gcol
        k_ref[hd] = jnp.where(rowid < QK_NOPE, tn, kr_t).astype(BF16)
    v = jnp.dot(c_kv, wv_ref[...], preferred_element_type=F32)
    for hd in range(N_HEADS):
        tile = v[:, hd * HEAD_TILE:(hd + 1) * HEAD_TILE]
        v_ref[hd] = jnp.where(lane == _ones_lane(hd % 2), 1.0, tile).astype(BF16)


def _ones_lane(parity):
    return V_DIM if parity == 0 else HEAD_TILE - V_DIM - 1


def _head_major_spec(heads, tm, seq):
    per_b = seq // tm
    return pl.BlockSpec((None, heads, tm, HEAD_TILE), lambda i: (i // per_b, 0, i % per_b, 0))


def _kv_proj(x2, pos2, inv_freq, gain, w_kv_a, kv_a_norm, w_kv_b, k_nope_norm, k_rope_norm, *,
             tm, bsz):
    n, d = x2.shape
    seq = n // bsz
    wa_rope = jnp.pad(w_kv_a[:, KV_LORA:], ((0, 0), (QK_NOPE, HEAD_TILE - QK_DIM)))
    wa = jnp.concatenate([w_kv_a[:, :KV_LORA], wa_rope], axis=1).astype(BF16)
    wb = w_kv_b.reshape(KV_LORA, N_HEADS, QK_NOPE + V_DIM)
    wk = jnp.pad(wb[:, :, :QK_NOPE], ((0, 0), (0, 0), (0, HEAD_TILE - QK_NOPE)))
    wk = wk.reshape(KV_LORA, N_HEADS * HEAD_TILE).T.astype(BF16)
    wv = wb[:, :, QK_NOPE:]
    odd_head = (jnp.arange(N_HEADS) % 2 == 1)[None, :, None]
    wv = jnp.where(odd_head, jnp.pad(wv, ((0, 0), (0, 0), (HEAD_TILE - V_DIM, 0))),
                   jnp.pad(wv, ((0, 0), (0, 0), (0, HEAD_TILE - V_DIM))))
    wv = wv.reshape(KV_LORA, N_HEADS * HEAD_TILE).astype(BF16)
    gain_tile = _head_tile_gain(k_nope_norm, k_rope_norm)
    row = lambda w: pl.BlockSpec((tm, w), lambda i: (i, 0))
    per_b = seq // tm
    kt_spec = pl.BlockSpec((None, N_HEADS, HEAD_TILE, tm), lambda i: (i // per_b, 0, 0, i % per_b))
    return pl.pallas_call(
        _kv_kernel,
        out_shape=[jax.ShapeDtypeStruct((bsz, N_HEADS, HEAD_TILE, seq), BF16),
                   jax.ShapeDtypeStruct((bsz, N_HEADS, seq, HEAD_TILE), BF16),
                   jax.ShapeDtypeStruct((n, HEAD_TILE), F32),
                   jax.ShapeDtypeStruct((n, HEAD_TILE), F32)],
        grid=(n // tm,),
        in_specs=[row(d), row(1), _const_spec((1, HEAD_TILE)), _const_spec((1, d)),
                  _const_spec(wa.shape), _const_spec((1, KV_LORA)), _const_spec(wk.shape),
                  _const_spec(wv.shape), _const_spec((HEAD_TILE, HEAD_TILE)),
                  _const_spec((HEAD_TILE, HEAD_TILE)),
                  _const_spec((1, HEAD_TILE)), _const_spec((HEAD_TILE, 1))],
        out_specs=[kt_spec, _head_major_spec(N_HEADS, tm, seq), row(HEAD_TILE), row(HEAD_TILE)],
        compiler_params=_params(("parallel",)),
        name="kv_proj",
    )(x2, pos2, _head_tile_freq(inv_freq), gain.reshape(1, d), wa, kv_a_norm.reshape(1, KV_LORA),
      wk, wv, _segment_mean_matrix(), _rope_swap_matrix(), gain_tile,
      gain_tile.reshape(HEAD_TILE, 1))


def _q_kernel(x_ref, cos_ref, sin_ref, g_ref, wa_ref, an_ref, wb_ref, e_ref, sw_ref, gt_ref, q_ref, *,
              scale):
    h = _rms(x_ref[...], g_ref[...]).astype(BF16)
    c_q = _rms(jnp.dot(h, wa_ref[...], preferred_element_type=F32), an_ref[...]).astype(BF16)
    cos2 = jnp.concatenate([cos_ref[...] * scale] * 2, axis=-1)
    sin2 = jnp.concatenate([sin_ref[...] * scale] * 2, axis=-1)
    gt2 = jnp.concatenate([gt_ref[...]] * 2, axis=-1)
    e = e_ref[...]
    swap = sw_ref[...]
    for pair in range(N_HEADS // 2):
        t2 = jnp.dot(c_q, wb_ref[:, 2 * pair * HEAD_TILE:(2 * pair + 2) * HEAD_TILE],
                     preferred_element_type=F32)
        ms2 = jnp.dot((t2 * t2).astype(BF16), e, preferred_element_type=F32)
        out = _rotate(t2 * lax.rsqrt(ms2 + EPS) * gt2, cos2, sin2, swap).astype(BF16)
        q_ref[2 * pair] = out[:, :HEAD_TILE]
        q_ref[2 * pair + 1] = out[:, HEAD_TILE:]


def _q_proj(x2, cos_t, sin_t, gain, w_q_a, q_a_norm, w_q_b, q_nope_norm, q_rope_norm, *, tm, bsz):
    n, d = x2.shape
    seq = n // bsz
    rank = w_q_a.shape[1]
    wb = w_q_b.reshape(rank, N_HEADS, QK_DIM)
    wb = jnp.pad(wb, ((0, 0), (0, 0), (0, HEAD_TILE - QK_DIM)))
    wb = wb.reshape(rank, N_HEADS * HEAD_TILE).astype(BF16)
    row = lambda w: pl.BlockSpec((tm, w), lambda i: (i, 0))
    return pl.pallas_call(
        functools.partial(_q_kernel, scale=math.log2(math.e) / math.sqrt(QK_DIM)),
        out_shape=jax.ShapeDtypeStruct((bsz, N_HEADS, seq, HEAD_TILE), BF16),
        grid=(n // tm,),
        in_specs=[row(d), row(HEAD_TILE), row(HEAD_TILE), _const_spec((1, d)),
                  _const_spec((d, rank)), _const_spec((1, rank)), _const_spec(wb.shape),
                  _const_spec((2 * HEAD_TILE, 2 * HEAD_TILE)),
                  _const_spec((2 * HEAD_TILE, 2 * HEAD_TILE)), _const_spec((1, HEAD_TILE))],
        out_specs=_head_major_spec(N_HEADS, tm, seq),
        compiler_params=_params(("parallel",)),
        name="q_proj",
    )(x2, cos_t, sin_t, gain.reshape(1, d), w_q_a.astype(BF16),
      q_a_norm.reshape(1, rank), wb, _segment_mean_matrix(2), _rope_swap_matrix(2),
      _head_tile_gain(q_nope_norm, q_rope_norm))


MAX_HEADS_IN_FLIGHT = 8
SCORE_SCRATCH_SEQS = 4


def _attn_kernel(q_ref, kt_ref, v_ref, o_ref, s_ref, m_ref, *, tq, nq):
    i = pl.program_id(1)
    nh = q_ref.shape[0]
    row = lax.broadcasted_iota(jnp.int32, (tq, tq), 0)
    col = lax.broadcasted_iota(jnp.int32, (tq, tq), 1)
    causal = col <= row
    lane_tiles = tq // LANES

    def stage_a(h, base, ms, nk):
        qh = q_ref[h]
        mx = None
        for j in range(nk):
            s = jnp.dot(qh, kt_ref[h, :, j * tq:(j + 1) * tq], preferred_element_type=F32)
            if j == nk - 1:
                s = jnp.where(causal, s, NEG_INF)
            s_ref[:, base + j * tq:base + (j + 1) * tq] = s
            for t in range(lane_tiles):
                f = s[:, t * LANES:(t + 1) * LANES]
                mx = f if mx is None else jnp.maximum(mx, f)
        m = jnp.max(mx, axis=-1, keepdims=True)
        m_ref[ms] = jnp.broadcast_to(m, (tq, LANES))

    def stage_b(h, base, ms, nk):
        m = m_ref[ms]
        m = jnp.concatenate([m] * lane_tiles, axis=-1)
        acc = None
        for j in range(nk):
            cols = slice(base + j * tq, base + (j + 1) * tq)
            p = jnp.exp2((s_ref[:, cols] - m).astype(BF16))
            pv = jnp.dot(p, v_ref[h, j * tq:(j + 1) * tq, :], preferred_element_type=F32)
            acc = pv if acc is None else acc + pv
        odd = (h % 2) == 1
        denom = jnp.where(odd, acc[:, _ones_lane(1):_ones_lane(1) + 1],
                          acc[:, _ones_lane(0):_ones_lane(0) + 1])
        o_ref[h] = (acc / denom).astype(BF16)

    for c in range(nq):
        @pl.when(i == c)
        def _(c=c):
            nk = c + 1
            hpi = min(m_ref.shape[0], s_ref.shape[1] // (nk * tq))
            hpi = 1 << (hpi.bit_length() - 1)

            def body(it, carry):
                for k in range(hpi):
                    stage_a(it * hpi + k, k * nk * tq, k, nk)
                for k in range(hpi):
                    stage_b(it * hpi + k, k * nk * tq, k, nk)
                return carry
            lax.fori_loop(0, nh // hpi, body, 0)


def _attention(q, kt, v, *, tq):
    bsz, nh, seq, _ = q.shape
    nq = seq // tq
    head_tile = lambda rows: pl.BlockSpec((None, nh, rows, HEAD_TILE), lambda b, i: (b, 0, i, 0))
    whole = lambda shape: pl.BlockSpec((None,) + shape, lambda b, i: (b, 0, 0, 0))
    return pl.pallas_call(
        functools.partial(_attn_kernel, tq=tq, nq=nq),
        out_shape=jax.ShapeDtypeStruct((bsz, nh, seq, HEAD_TILE), BF16),
        grid=(bsz, nq),
        in_specs=[head_tile(tq), whole((nh, HEAD_TILE, seq)), whole((nh, seq, HEAD_TILE))],
        out_specs=head_tile(tq),
        scratch_shapes=[pltpu.VMEM((tq, SCORE_SCRATCH_SEQS * seq), F32),
                        pltpu.VMEM((MAX_HEADS_IN_FLIGHT, tq, LANES), F32)],
        compiler_params=_params(("parallel", "arbitrary")),
        name="causal_attention",
    )(q, kt, v)


def _tiles(bsz, seq):
    tm = 512 if seq % 512 == 0 else seq
    tc = 16 if seq % 16 == 0 else seq
    tq = 256 if seq % 256 == 0 else seq
    return tm, tc, tq


def kernel(x, positions, mix_norm, ffn_norm, ffn_w_gate_up, ffn_w_down, ssm_w_in, ssm_lambda_re, ssm_lambda_im, ssm_log_step, ssm_b_re, ssm_b_im, ssm_c_re, ssm_c_im, ssm_d, ssm_w_glu, kv_in_norm, mla_w_kv_a, mla_kv_a_norm, mla_w_kv_b, mla_k_nope_norm, mla_k_rope_norm, mla_w_q_a, mla_q_a_norm, mla_w_q_b, mla_q_nope_norm, mla_q_rope_norm, mla_w_o):
    bsz, seq, d = x.shape
    n = bsz * seq
    depth = mix_norm.shape[0]
    n_ssm = ssm_w_in.shape[0]
    assert bsz % 8 == 0 and d % CH_TILE == 0
    tm, tc, tq = _tiles(bsz, seq)

    xs = x
    for i in range(n_ssm):
        a_re, a_im, br, bi = _discretise(ssm_lambda_re[i], ssm_lambda_im[i], ssm_log_step[i],
                                         ssm_b_re[i], ssm_b_im[i])
        b_blk, c_blk = _block_diag_weights(br, bi, ssm_c_re[i], ssm_c_im[i])
        xs = _s5_layer(xs, mix_norm[i], ssm_w_in[i], b_blk, a_re, a_im, c_blk, ssm_d[i],
                       ssm_w_glu[i], tc=tc, batch_major_in=(i == 0))
        xs = _ffn_layer(xs.reshape(n, d), ffn_norm[i], ffn_w_gate_up[i], ffn_w_down[i], tm=tm,
                        batch_major_out=bsz if i == n_ssm - 1 else None)
        if i < n_ssm - 1:
            xs = xs.reshape(seq, bsz, d)
    x2 = xs.reshape(n, d)

    pos2 = positions.reshape(n, 1)
    inv_freq = ROPE_THETA ** (-jnp.arange(0, QK_ROPE, 2, dtype=F32) / QK_ROPE)
    k, v, cos_t, sin_t = _kv_proj(x2, pos2, inv_freq, kv_in_norm, mla_w_kv_a, mla_kv_a_norm,
                                  mla_w_kv_b, mla_k_nope_norm, mla_k_rope_norm, tm=tm, bsz=bsz)
    for layer in range(n_ssm, depth):
        j = layer - n_ssm
        q = _q_proj(x2, cos_t, sin_t, mix_norm[layer], mla_w_q_a[j], mla_q_a_norm[j],
                    mla_w_q_b[j], mla_q_nope_norm[j], mla_q_rope_norm[j], tm=tm, bsz=bsz)
        o = _attention(q, k, v, tq=tq)
        x2 = _ffn_layer(x2, ffn_norm[layer], ffn_w_gate_up[layer], ffn_w_down[layer], tm=tm,
                        attn=o, w_o=mla_w_o[j])
    return x2.reshape(bsz, seq, d)
```

```python
import functools
import math

import jax
import jax.numpy as jnp
from jax import lax
from jax.experimental import pallas as pl
from jax.experimental.pallas import tpu as pltpu

F32 = jnp.float32
BF16 = jnp.bfloat16

EPS = 1e-6
NEG_INF = -1e30
ROPE_THETA = 10000.0

SSM_GROUP = 16
SSM_STATE = 64
N_HEADS = 16
QK_NOPE = 64
QK_ROPE = 32
HALF_ROPE = QK_ROPE // 2
QK_DIM = QK_NOPE + QK_ROPE
V_DIM = 64
KV_LORA = 256

LANES = 128
MXU_TILE = 256
VMEM_LIMIT = 56 * 1024 * 1024

HEAD_TILE = LANES
CH_TILE = MXU_TILE
GROUPS_PER_TILE = CH_TILE // SSM_GROUP
STATES_PER_TILE = GROUPS_PER_TILE * SSM_STATE
SCAN_LANES = LANES


def _rms(x, gain):
    ms = jnp.mean(x * x, axis=-1, keepdims=True)
    return x * lax.rsqrt(ms + EPS) * gain


def _const_spec(shape):
    nd = len(shape)
    return pl.BlockSpec(shape, lambda *_: (0,) * nd, pipeline_mode=pl.Buffered(1))


def _params(semantics):
    return pltpu.CompilerParams(dimension_semantics=semantics, vmem_limit_bytes=VMEM_LIMIT)


def _disc_kernel(lr_ref, li_ref, ls_ref, bre_ref, bim_ref, are_ref, aim_ref, br_ref, bi_ref):
    lr = lr_ref[...]
    li = li_ref[...]
    step = jnp.exp(ls_ref[...])
    decay = jnp.exp(lr * step)
    a_re = decay * jnp.cos(li * step)
    a_im = decay * jnp.sin(li * step)
    den = lr * lr + li * li
    f_re = ((a_re - 1.0) * lr + a_im * li) / den
    f_im = (a_im * lr - (a_re - 1.0) * li) / den
    b_re = bre_ref[...]
    b_im = bim_ref[...]
    are_ref[...] = a_re
    aim_ref[...] = a_im
    br_ref[...] = f_re * b_re - f_im * b_im
    bi_ref[...] = f_re * b_im + f_im * b_re


def _discretise(lam_re, lam_im, log_step, b_re, b_im):
    g, n = lam_re.shape
    k = b_re.shape[-1]
    gn = g * n
    col = lambda a: a.reshape(gn, 1)
    ls = jnp.broadcast_to(log_step[:, None], (g, n))
    outs = pl.pallas_call(
        _disc_kernel,
        out_shape=[jax.ShapeDtypeStruct((gn, 1), F32), jax.ShapeDtypeStruct((gn, 1), F32),
                   jax.ShapeDtypeStruct((gn, k), F32), jax.ShapeDtypeStruct((gn, k), F32)],
        name="s5_discretise",
    )(col(lam_re), col(lam_im), col(ls), b_re.reshape(gn, k), b_im.reshape(gn, k))
    a_re, a_im, br, bi = outs
    return a_re.reshape(g, n), a_im.reshape(g, n), br.reshape(g, n, k), bi.reshape(g, n, k)


def _block_diag_weights(br, bi, c_re, c_im):
    g, n, k = br.shape
    tiles = g // GROUPS_PER_TILE
    eye = jnp.eye(GROUPS_PER_TILE, dtype=F32)

    def bmat(b):
        b4 = b.reshape(tiles, GROUPS_PER_TILE, n, k)
        return jnp.einsum('qgnk,gh->qgkhn', b4, eye).reshape(tiles, CH_TILE, STATES_PER_TILE)

    def cmat(c):
        c4 = c.reshape(tiles, GROUPS_PER_TILE, k, n)
        return jnp.einsum('qgkn,gh->qgnhk', c4, eye).reshape(tiles, STATES_PER_TILE, CH_TILE)

    chunks = STATES_PER_TILE // SCAN_LANES
    b_blk = jnp.stack([bmat(br).reshape(tiles, CH_TILE, chunks, SCAN_LANES),
                       bmat(bi).reshape(tiles, CH_TILE, chunks, SCAN_LANES)], axis=3)
    c_blk = jnp.stack([cmat(c_re).reshape(tiles, chunks, SCAN_LANES, CH_TILE),
                       -cmat(c_im).reshape(tiles, chunks, SCAN_LANES, CH_TILE)], axis=2)
    return (b_blk.reshape(tiles, CH_TILE, 2 * STATES_PER_TILE).astype(BF16),
            c_blk.reshape(tiles, 2 * STATES_PER_TILE, CH_TILE).astype(BF16))


def _s5_kernel(x_ref, g_ref, win_ref, b_ref, are_ref, aim_ref, c_ref, d_ref, wglu_ref,
               o_ref, st_ref, bu_ref, z_ref, *, tc, nb, d_model, batch_major_in):
    rows = tc * nb
    tiles = d_model // CH_TILE

    @pl.when(pl.program_id(0) == 0)
    def _():
        st_ref[...] = jnp.zeros_like(st_ref)

    x = x_ref[...]
    if batch_major_in:
        x = pltpu.einshape("btd->tbd", x)
    x = x.reshape(rows, d_model)
    h = _rms(x, g_ref[...]).astype(BF16)
    u = jnp.dot(h, win_ref[...], preferred_element_type=F32)
    u_bf = u.astype(BF16)

    for q in range(tiles):
        ch = slice(q * CH_TILE, (q + 1) * CH_TILE)
        y = d_ref[:, ch] * u[:, ch]
        for c in range(STATES_PER_TILE // SCAN_LANES):
            cols = slice(2 * c * SCAN_LANES, 2 * (c + 1) * SCAN_LANES)
            re = slice(2 * c * SCAN_LANES, (2 * c + 1) * SCAN_LANES)
            im = slice((2 * c + 1) * SCAN_LANES, (2 * c + 2) * SCAN_LANES)
            st = slice(c * SCAN_LANES, (c + 1) * SCAN_LANES)
            bu_ref[:, cols] = jnp.dot(u_bf[:, ch], b_ref[q, :, cols], preferred_element_type=F32)
            ar = jnp.broadcast_to(are_ref[q, :, st], (nb, SCAN_LANES))
            ai = jnp.broadcast_to(aim_ref[q, :, st], (nb, SCAN_LANES))
            sr, si = st_ref[q, :, re], st_ref[q, :, im]
            for t in range(tc):
                rws = slice(t * nb, (t + 1) * nb)
                nr = ar * sr - ai * si + bu_ref[rws, re]
                ni = ar * si + ai * sr + bu_ref[rws, im]
                bu_ref[rws, re] = nr
                bu_ref[rws, im] = ni
                sr, si = nr, ni
            st_ref[q, :, re] = sr
            st_ref[q, :, im] = si
            y = y + jnp.dot(bu_ref[:, cols].astype(BF16), c_ref[q, cols, :],
                            preferred_element_type=F32)
        z_ref[:, ch] = jax.nn.gelu(y).astype(BF16)

    glu = jnp.dot(z_ref[...], wglu_ref[...], preferred_element_type=F32)
    out = x + glu[:, :d_model] * jax.nn.sigmoid(glu[:, d_model:])
    o_ref[...] = out.reshape(tc, nb, d_model)


def _s5_layer(xin, gain, w_in, b_blk, a_re, a_im, c_blk, d_skip, w_glu, *, tc, batch_major_in):
    if batch_major_in:
        nb, seq, d = xin.shape
        x_spec = pl.BlockSpec((nb, tc, d), lambda i: (0, i, 0))
    else:
        seq, nb, d = xin.shape
        x_spec = pl.BlockSpec((tc, nb, d), lambda i: (i, 0, 0))
    tiles = d // CH_TILE
    rows = tc * nb
    kern = functools.partial(_s5_kernel, tc=tc, nb=nb, d_model=d, batch_major_in=batch_major_in)
    return pl.pallas_call(
        kern,
        out_shape=jax.ShapeDtypeStruct((seq, nb, d), F32),
        grid=(seq // tc,),
        in_specs=[
            x_spec,
            _const_spec((1, d)),
            _const_spec((d, d)),
            _const_spec((tiles, CH_TILE, 2 * STATES_PER_TILE)),
            _const_spec((tiles, 1, STATES_PER_TILE)),
            _const_spec((tiles, 1, STATES_PER_TILE)),
            _const_spec((tiles, 2 * STATES_PER_TILE, CH_TILE)),
            _const_spec((1, d)),
            _const_spec((d, 2 * d)),
        ],
        out_specs=pl.BlockSpec((tc, nb, d), lambda i: (i, 0, 0)),
        scratch_shapes=[
            pltpu.VMEM((tiles, nb, 2 * STATES_PER_TILE), F32),
            pltpu.VMEM((rows, 2 * STATES_PER_TILE), F32),
            pltpu.VMEM((rows, d), BF16),
        ],
        compiler_params=_params(("arbitrary",)),
        name="s5_layer",
    )(xin, gain.reshape(1, d), w_in.astype(BF16), b_blk,
      a_re.reshape(tiles, 1, STATES_PER_TILE), a_im.reshape(tiles, 1, STATES_PER_TILE),
      c_blk, d_skip.reshape(1, d), w_glu.astype(BF16))


FF_CHUNK = MXU_TILE


def _ffn_body(x, g_ref, wgu_ref, wd_ref, d_ff):
    h = _rms(x, g_ref[...]).astype(BF16)
    acc = jnp.zeros(x.shape, F32)
    for c in range(d_ff // FF_CHUNK):
        gate = jnp.dot(h, wgu_ref[:, c * FF_CHUNK:(c + 1) * FF_CHUNK], preferred_element_type=F32)
        up = jnp.dot(h, wgu_ref[:, d_ff + c * FF_CHUNK:d_ff + (c + 1) * FF_CHUNK],
                     preferred_element_type=F32)
        act = (jax.nn.silu(gate) * up).astype(BF16)
        acc = acc + jnp.dot(act, wd_ref[c * FF_CHUNK:(c + 1) * FF_CHUNK, :],
                            preferred_element_type=F32)
    return x + acc


def _ffn_kernel(x_ref, g_ref, wgu_ref, wd_ref, o_ref, *, d_ff):
    out = _ffn_body(x_ref[...], g_ref, wgu_ref, wd_ref, d_ff)
    if len(o_ref.shape) == 3:
        nb, steps, d = o_ref.shape
        out = pltpu.einshape("tbd->btd", out.reshape(steps, nb, d))
    o_ref[...] = out


def _proj_ffn_kernel(x_ref, a_ref, wo_ref, g_ref, wgu_ref, wd_ref, o_ref, *, d_ff):
    lane = lax.broadcasted_iota(jnp.int32, a_ref.shape[1:], 1)
    attn = jnp.concatenate([jnp.where(lane < V_DIM, a_ref[2 * p], a_ref[2 * p + 1])
                            for p in range(a_ref.shape[0] // 2)], axis=-1)
    x = x_ref[...] + jnp.dot(attn, wo_ref[...], preferred_element_type=F32)
    o_ref[...] = _ffn_body(x, g_ref, wgu_ref, wd_ref, d_ff)


def _ffn_layer(x2, gain, w_gate_up, w_down, *, tm, attn=None, w_o=None, batch_major_out=None):
    n, d = x2.shape
    d_ff = w_down.shape[0]
    row_spec = pl.BlockSpec((tm, d), lambda i: (i, 0))
    out_shape, out_spec = (n, d), row_spec
    if batch_major_out is not None:
        nb = batch_major_out
        out_shape = (nb, n // nb, d)
        out_spec = pl.BlockSpec((nb, tm // nb, d), lambda i: (0, i, 0))
    w_specs = [_const_spec((1, d)), _const_spec((d, 2 * d_ff)), _const_spec((d_ff, d))]
    w_args = (gain.reshape(1, d), w_gate_up.astype(BF16), w_down.astype(BF16))
    if attn is None:
        kern, in_specs, args = _ffn_kernel, [row_spec] + w_specs, (x2,) + w_args
    else:
        _, heads, seq, width = attn.shape
        per_b = seq // tm
        kern = _proj_ffn_kernel
        attn_spec = pl.BlockSpec((None, heads, tm, width), lambda i: (i // per_b, 0, i % per_b, 0))
        in_specs = [row_spec, attn_spec, _const_spec((heads // 2 * width, d))] + w_specs
        args = (x2, attn, w_o.astype(BF16)) + w_args
    return pl.pallas_call(
        functools.partial(kern, d_ff=d_ff),
        out_shape=jax.ShapeDtypeStruct(out_shape, F32),
        grid=(n // tm,),
        in_specs=in_specs,
        out_specs=out_spec,
        compiler_params=_params(("parallel",)),
        name="ffn_layer",
    )(*args)


def _rope_tables(pos_ref, freq_ref):
    ang = pos_ref[...].astype(F32) * freq_ref[...]
    return jnp.cos(ang), jnp.sin(ang)


def _rotate(tn, cos_t, sin_t, swap):
    partner = jnp.dot(tn.astype(BF16), swap, preferred_element_type=F32)
    return tn * cos_t + partner * sin_t


def _rope_swap_matrix(tiles=1):
    src = jnp.arange(HEAD_TILE)[:, None]
    dst = jnp.arange(HEAD_TILE)[None, :]
    x1_col = (dst >= QK_NOPE) & (dst < QK_NOPE + HALF_ROPE)
    x2_col = (dst >= QK_NOPE + HALF_ROPE) & (dst < QK_DIM)
    p = (jnp.where(x1_col & (src == dst + HALF_ROPE), -1.0, 0.0)
         + jnp.where(x2_col & (src == dst - HALF_ROPE), 1.0, 0.0))
    return jnp.kron(jnp.eye(tiles, dtype=F32), p).astype(BF16)


def _segment_mean_matrix(tiles=1):
    idx = jnp.arange(HEAD_TILE)
    seg = jnp.where(idx < QK_NOPE, 0, jnp.where(idx < QK_DIM, 1, 2))
    same = (seg[:, None] == seg[None, :]) & (seg[:, None] < 2)
    width = jnp.where(seg == 0, QK_NOPE, QK_ROPE).astype(F32)
    e = jnp.where(same, 1.0 / width[None, :], 0.0)
    return jnp.kron(jnp.eye(tiles, dtype=F32), e).astype(BF16)


def _head_tile_freq(inv_freq):
    z = jnp.zeros((QK_NOPE,), F32)
    pad = jnp.zeros((HEAD_TILE - QK_DIM,), F32)
    return jnp.concatenate([z, inv_freq, inv_freq, pad]).reshape(1, HEAD_TILE)


def _head_tile_gain(nope_gain, rope_gain):
    pad = jnp.zeros((HEAD_TILE - QK_DIM,), F32)
    return jnp.concatenate([nope_gain, rope_gain, pad]).reshape(1, HEAD_TILE)


def _kv_kernel(x_ref, pos_ref, freq_ref, g_ref, wa_ref, an_ref, wk_ref, wv_ref, e_ref, sw_ref,
               gt_ref, gc_ref, k_ref, v_ref, cos_ref, sin_ref):
    h = _rms(x_ref[...], g_ref[...]).astype(BF16)
    ckv = jnp.dot(h, wa_ref[...], preferred_element_type=F32)
    c_kv = _rms(ckv[:, :KV_LORA], an_ref[...]).astype(BF16)
    cos_t, sin_t = _rope_tables(pos_ref, freq_ref)
    cos_ref[...] = cos_t
    sin_ref[...] = sin_t
    gt = gt_ref[...]
    e = e_ref[...]

    kr = ckv[:, KV_LORA:]
    ms = jnp.dot((kr * kr).astype(BF16), e, preferred_element_type=F32)
    kr = _rotate(kr * lax.rsqrt(ms + EPS) * gt, cos_t, sin_t, sw_ref[...])
    lane = lax.broadcasted_iota(jnp.int32, kr.shape, 1)
    tm = kr.shape[0]

    kr_t = kr.T
    kn_t = lax.dot_general(wk_ref[...], c_kv, (((1,), (1,)), ((), ())),
                           preferred_element_type=F32)
    rowid = lax.broadcasted_iota(jnp.int32, (HEAD_TILE, tm), 0)
    gcol = jnp.broadcast_to(gc_ref[...], (HEAD_TILE, tm))
    for hd in range(N_HEADS):
        t = kn_t[hd * HEAD_TILE:(hd + 1) * HEAD_TILE, :]
        nope = t[:QK_NOPE]
        ms = jnp.sum(nope * nope, axis=0, keepdims=True) * (1.0 / QK_NOPE)
        tn = t * lax.rsqrt(ms + EPS) * gcol
        k_ref[hd] = jnp.where(rowid < QK_NOPE, tn, kr_t).astype(BF16)
    v = jnp.dot(c_kv, wv_ref[...], preferred_element_type=F32)
    for hd in range(N_HEADS):
        tile = v[:, hd * HEAD_TILE:(hd + 1) * HEAD_TILE]
        v_ref[hd] = jnp.where(lane == _ones_lane(hd % 2), 1.0, tile).astype(BF16)


def _ones_lane(parity):
    return V_DIM if parity == 0 else HEAD_TILE - V_DIM - 1


def _head_major_spec(heads, tm, seq):
    per_b = seq // tm
    return pl.BlockSpec((None, heads, tm, HEAD_TILE), lambda i: (i // per_b, 0, i % per_b, 0))


def _kv_proj(x2, pos2, inv_freq, gain, w_kv_a, kv_a_norm, w_kv_b, k_nope_norm, k_rope_norm, *,
             tm, bsz):
    n, d = x2.shape
    seq = n // bsz
    wa_rope = jnp.pad(w_kv_a[:, KV_LORA:], ((0, 0), (QK_NOPE, HEAD_TILE - QK_DIM)))
    wa = jnp.concatenate([w_kv_a[:, :KV_LORA], wa_rope], axis=1).astype(BF16)
    wb = w_kv_b.reshape(KV_LORA, N_HEADS, QK_NOPE + V_DIM)
    wk = jnp.pad(wb[:, :, :QK_NOPE], ((0, 0), (0, 0), (0, HEAD_TILE - QK_NOPE)))
    wk = wk.reshape(KV_LORA, N_HEADS * HEAD_TILE).T.astype(BF16)
    wv = wb[:, :, QK_NOPE:]
    odd_head = (jnp.arange(N_HEADS) % 2 == 1)[None, :, None]
    wv = jnp.where(odd_head, jnp.pad(wv, ((0, 0), (0, 0), (HEAD_TILE - V_DIM, 0))),
                   jnp.pad(wv, ((0, 0), (0, 0), (0, HEAD_TILE - V_DIM))))
    wv = wv.reshape(KV_LORA, N_HEADS * HEAD_TILE).astype(BF16)
    gain_tile = _head_tile_gain(k_nope_norm, k_rope_norm)
    row = lambda w: pl.BlockSpec((tm, w), lambda i: (i, 0))
    per_b = seq // tm
    kt_spec = pl.BlockSpec((None, N_HEADS, HEAD_TILE, tm), lambda i: (i // per_b, 0, 0, i % per_b))
    return pl.pallas_call(
        _kv_kernel,
        out_shape=[jax.ShapeDtypeStruct((bsz, N_HEADS, HEAD_TILE, seq), BF16),
                   jax.ShapeDtypeStruct((bsz, N_HEADS, seq, HEAD_TILE), BF16),
                   jax.ShapeDtypeStruct((n, HEAD_TILE), F32),
                   jax.ShapeDtypeStruct((n, HEAD_TILE), F32)],
        grid=(n // tm,),
        in_specs=[row(d), row(1), _const_spec((1, HEAD_TILE)), _const_spec((1, d)),
                  _const_spec(wa.shape), _const_spec((1, KV_LORA)), _const_spec(wk.shape),
                  _const_spec(wv.shape), _const_spec((HEAD_TILE, HEAD_TILE)),
                  _const_spec((HEAD_TILE, HEAD_TILE)),
                  _const_spec((1, HEAD_TILE)), _const_spec((HEAD_TILE, 1))],
        out_specs=[kt_spec, _head_major_spec(N_HEADS, tm, seq), row(HEAD_TILE), row(HEAD_TILE)],
        compiler_params=_params(("parallel",)),
        name="kv_proj",
    )(x2, pos2, _head_tile_freq(inv_freq), gain.reshape(1, d), wa, kv_a_norm.reshape(1, KV_LORA),
      wk, wv, _segment_mean_matrix(), _rope_swap_matrix(), gain_tile,
      gain_tile.reshape(HEAD_TILE, 1))


def _q_kernel(x_ref, cos_ref, sin_ref, g_ref, wa_ref, an_ref, wb_ref, e_ref, sw_ref, gt_ref, q_ref, *,
              scale):
    h = _rms(x_ref[...], g_ref[...]).astype(BF16)
    c_q = _rms(jnp.dot(h, wa_ref[...], preferred_element_type=F32), an_ref[...]).astype(BF16)
    cos2 = jnp.concatenate([cos_ref[...] * scale] * 2, axis=-1)
    sin2 = jnp.concatenate([sin_ref[...] * scale] * 2, axis=-1)
    gt2 = jnp.concatenate([gt_ref[...]] * 2, axis=-1)
    e = e_ref[...]
    swap = sw_ref[...]
    for pair in range(N_HEADS // 2):
        t2 = jnp.dot(c_q, wb_ref[:, 2 * pair * HEAD_TILE:(2 * pair + 2) * HEAD_TILE],
                     preferred_element_type=F32)
        ms2 = jnp.dot((t2 * t2).astype(BF16), e, preferred_element_type=F32)
        out = _rotate(t2 * lax.rsqrt(ms2 + EPS) * gt2, cos2, sin2, swap).astype(BF16)
        q_ref[2 * pair] = out[:, :HEAD_TILE]
        q_ref[2 * pair + 1] = out[:, HEAD_TILE:]


def _q_proj(x2, cos_t, sin_t, gain, w_q_a, q_a_norm, w_q_b, q_nope_norm, q_rope_norm, *, tm, bsz):
    n, d = x2.shape
    seq = n // bsz
    rank = w_q_a.shape[1]
    wb = w_q_b.reshape(rank, N_HEADS, QK_DIM)
    wb = jnp.pad(wb, ((0, 0), (0, 0), (0, HEAD_TILE - QK_DIM)))
    wb = wb.reshape(rank, N_HEADS * HEAD_TILE).astype(BF16)
    row = lambda w: pl.BlockSpec((tm, w), lambda i: (i, 0))
    return pl.pallas_call(
        functools.partial(_q_kernel, scale=math.log2(math.e) / math.sqrt(QK_DIM)),
        out_shape=jax.ShapeDtypeStruct((bsz, N_HEADS, seq, HEAD_TILE), BF16),
        grid=(n // tm,),
        in_specs=[row(d), row(HEAD_TILE), row(HEAD_TILE), _const_spec((1, d)),
                  _const_spec((d, rank)), _const_spec((1, rank)), _const_spec(wb.shape),
                  _const_spec((2 * HEAD_TILE, 2 * HEAD_TILE)),
                  _const_spec((2 * HEAD_TILE, 2 * HEAD_TILE)), _const_spec((1, HEAD_TILE))],
        out_specs=_head_major_spec(N_HEADS, tm, seq),
        compiler_params=_params(("parallel",)),
        name="q_proj",
    )(x2, cos_t, sin_t, gain.reshape(1, d), w_q_a.astype(BF16),
      q_a_norm.reshape(1, rank), wb, _segment_mean_matrix(2), _rope_swap_matrix(2),
      _head_tile_gain(q_nope_norm, q_rope_norm))


MAX_HEADS_IN_FLIGHT = 8
SCORE_SCRATCH_SEQS = 6


def _attn_kernel(q_ref, kt_ref, v_ref, o_ref, s_ref, m_ref, *, tq, nq):
    i = pl.program_id(1)
    nh = q_ref.shape[0]
    row = lax.broadcasted_iota(jnp.int32, (tq, tq), 0)
    col = lax.broadcasted_iota(jnp.int32, (tq, tq), 1)
    causal = col <= row
    lane_tiles = tq // LANES

    def stage_a(h, base, ms, nk):
        qh = q_ref[h]
        mx = None
        for j in range(nk):
            s = jnp.dot(qh, kt_ref[h, :, j * tq:(j + 1) * tq], preferred_element_type=F32)
            if j == nk - 1:
                s = jnp.where(causal, s, NEG_INF)
            s_ref[:, base + j * tq:base + (j + 1) * tq] = s
            for t in range(lane_tiles):
                f = s[:, t * LANES:(t + 1) * LANES]
                mx = f if mx is None else jnp.maximum(mx, f)
        m = jnp.max(mx, axis=-1, keepdims=True)
        m_ref[ms] = jnp.broadcast_to(m, (tq, LANES))

    def stage_b(h, base, ms, nk):
        m = m_ref[ms]
        m = jnp.concatenate([m] * lane_tiles, axis=-1)
        acc = None
        for j in range(nk):
            cols = slice(base + j * tq, base + (j + 1) * tq)
            p = jnp.exp2((s_ref[:, cols] - m).astype(BF16))
            pv = jnp.dot(p, v_ref[h, j * tq:(j + 1) * tq, :], preferred_element_type=F32)
            acc = pv if acc is None else acc + pv
        odd = (h % 2) == 1
        denom = jnp.where(odd, acc[:, _ones_lane(1):_ones_lane(1) + 1],
                          acc[:, _ones_lane(0):_ones_lane(0) + 1])
        o_ref[h] = (acc / denom).astype(BF16)

    for c in range(nq):
        @pl.when(i == c)
        def _(c=c):
            nk = c + 1
            hpi = min(m_ref.shape[0], s_ref.shape[1] // (nk * tq))
            hpi = 1 << (hpi.bit_length() - 1)

            def body(it, carry):
                for k in range(hpi):
                    stage_a(it * hpi + k, k * nk * tq, k, nk)
                for k in range(hpi):
                    stage_b(it * hpi + k, k * nk * tq, k, nk)
                return carry
            lax.fori_loop(0, nh // hpi, body, 0)


def _attention(q, kt, v, *, tq):
    bsz, nh, seq, _ = q.shape
    nq = seq // tq
    head_tile = lambda rows: pl.BlockSpec((None, nh, rows, HEAD_TILE), lambda b, i: (b, 0, i, 0))
    whole = lambda shape: pl.BlockSpec((None,) + shape, lambda b, i: (b, 0, 0, 0))
    return pl.pallas_call(
        functools.partial(_attn_kernel, tq=tq, nq=nq),
        out_shape=jax.ShapeDtypeStruct((bsz, nh, seq, HEAD_TILE), BF16),
        grid=(bsz, nq),
        in_specs=[head_tile(tq), whole((nh, HEAD_TILE, seq)), whole((nh, seq, HEAD_TILE))],
        out_specs=head_tile(tq),
        scratch_shapes=[pltpu.VMEM((tq, SCORE_SCRATCH_SEQS * seq), F32),
                        pltpu.VMEM((MAX_HEADS_IN_FLIGHT, tq, LANES), F32)],
        compiler_params=_params(("parallel", "arbitrary")),
        name="causal_attention",
    )(q, kt, v)


def _tiles(bsz, seq):
    tm = 512 if seq % 512 == 0 else seq
    tp = 1024 if seq % 1024 == 0 else tm
    tc = 16 if seq % 16 == 0 else seq
    tq = 256 if seq % 256 == 0 else seq
    assert tm % bsz == 0 and (tm // bsz) % 8 == 0, "an FFN row tile holds whole, sublane-aligned time steps"
    return tm, tp, tc, tq


def kernel(x, positions, mix_norm, ffn_norm, ffn_w_gate_up, ffn_w_down, ssm_w_in, ssm_lambda_re, ssm_lambda_im, ssm_log_step, ssm_b_re, ssm_b_im, ssm_c_re, ssm_c_im, ssm_d, ssm_w_glu, kv_in_norm, mla_w_kv_a, mla_kv_a_norm, mla_w_kv_b, mla_k_nope_norm, mla_k_rope_norm, mla_w_q_a, mla_q_a_norm, mla_w_q_b, mla_q_nope_norm, mla_q_rope_norm, mla_w_o):
    bsz, seq, d = x.shape
    n = bsz * seq
    depth = mix_norm.shape[0]
    n_ssm = ssm_w_in.shape[0]
    assert bsz % 8 == 0 and d % CH_TILE == 0
    tm, tp, tc, tq = _tiles(bsz, seq)

    xs = x
    for i in range(n_ssm):
        a_re, a_im, br, bi = _discretise(ssm_lambda_re[i], ssm_lambda_im[i], ssm_log_step[i],
                                         ssm_b_re[i], ssm_b_im[i])
        b_blk, c_blk = _block_diag_weights(br, bi, ssm_c_re[i], ssm_c_im[i])
        xs = _s5_layer(xs, mix_norm[i], ssm_w_in[i], b_blk, a_re, a_im, c_blk, ssm_d[i],
                       ssm_w_glu[i], tc=tc, batch_major_in=(i == 0))
        xs = _ffn_layer(xs.reshape(n, d), ffn_norm[i], ffn_w_gate_up[i], ffn_w_down[i], tm=tm,
                        batch_major_out=bsz if i == n_ssm - 1 else None)
        if i < n_ssm - 1:
            xs = xs.reshape(seq, bsz, d)
    x2 = xs.reshape(n, d)

    pos2 = positions.reshape(n, 1)
    inv_freq = ROPE_THETA ** (-jnp.arange(0, QK_ROPE, 2, dtype=F32) / QK_ROPE)
    k, v, cos_t, sin_t = _kv_proj(x2, pos2, inv_freq, kv_in_norm, mla_w_kv_a, mla_kv_a_norm,
                                  mla_w_kv_b, mla_k_nope_norm, mla_k_rope_norm, tm=tp, bsz=bsz)
    for layer in range(n_ssm, depth):
        j = layer - n_ssm
        q = _q_proj(x2, cos_t, sin_t, mix_norm[layer], mla_w_q_a[j], mla_q_a_norm[j],
                    mla_w_q_b[j], mla_q_nope_norm[j], mla_q_rope_norm[j], tm=tp, bsz=bsz)
        o = _attention(q, k, v, tq=tq)
        x2 = _ffn_layer(x2, ffn_norm[layer], ffn_w_gate_up[layer], ffn_w_down[layer], tm=tm,
                        attn=o, w_o=mla_w_o[j])
    return x2.reshape(bsz, seq, d)
```

```python
import functools
import math

import jax
import jax.numpy as jnp
from jax import lax
from jax.experimental import pallas as pl
from jax.experimental.pallas import tpu as pltpu

F32 = jnp.float32
BF16 = jnp.bfloat16

EPS = 1e-6
NEG_INF = -1e30
ROPE_THETA = 10000.0

SSM_GROUP = 16
SSM_STATE = 64
N_HEADS = 16
QK_NOPE = 64
QK_ROPE = 32
HALF_ROPE = QK_ROPE // 2
QK_DIM = QK_NOPE + QK_ROPE
V_DIM = 64
KV_LORA = 256

LANES = 128
MXU_TILE = 256
VMEM_LIMIT = 56 * 1024 * 1024

HEAD_TILE = LANES
CH_TILE = MXU_TILE
GROUPS_PER_TILE = CH_TILE // SSM_GROUP
STATES_PER_TILE = GROUPS_PER_TILE * SSM_STATE
SCAN_LANES = LANES


def _rms(x, gain):
    ms = jnp.mean(x * x, axis=-1, keepdims=True)
    return x * lax.rsqrt(ms + EPS) * gain


def _const_spec(shape):
    nd = len(shape)
    return pl.BlockSpec(shape, lambda *_: (0,) * nd, pipeline_mode=pl.Buffered(1))


def _params(semantics):
    return pltpu.CompilerParams(dimension_semantics=semantics, vmem_limit_bytes=VMEM_LIMIT)


def _disc_kernel(lr_ref, li_ref, ls_ref, bre_ref, bim_ref, are_ref, aim_ref, br_ref, bi_ref):
    lr = lr_ref[...]
    li = li_ref[...]
    step = jnp.exp(ls_ref[...])
    decay = jnp.exp(lr * step)
    a_re = decay * jnp.cos(li * step)
    a_im = decay * jnp.sin(li * step)
    den = lr * lr + li * li
    f_re = ((a_re - 1.0) * lr + a_im * li) / den
    f_im = (a_im * lr - (a_re - 1.0) * li) / den
    b_re = bre_ref[...]
    b_im = bim_ref[...]
    are_ref[...] = a_re
    aim_ref[...] = a_im
    br_ref[...] = f_re * b_re - f_im * b_im
    bi_ref[...] = f_re * b_im + f_im * b_re


def _discretise(lam_re, lam_im, log_step, b_re, b_im):
    g, n = lam_re.shape
    k = b_re.shape[-1]
    gn = g * n
    col = lambda a: a.reshape(gn, 1)
    ls = jnp.broadcast_to(log_step[:, None], (g, n))
    outs = pl.pallas_call(
        _disc_kernel,
        out_shape=[jax.ShapeDtypeStruct((gn, 1), F32), jax.ShapeDtypeStruct((gn, 1), F32),
                   jax.ShapeDtypeStruct((gn, k), F32), jax.ShapeDtypeStruct((gn, k), F32)],
        name="s5_discretise",
    )(col(lam_re), col(lam_im), col(ls), b_re.reshape(gn, k), b_im.reshape(gn, k))
    a_re, a_im, br, bi = outs
    return a_re.reshape(g, n), a_im.reshape(g, n), br.reshape(g, n, k), bi.reshape(g, n, k)


def _block_diag_weights(br, bi, c_re, c_im):
    g, n, k = br.shape
    tiles = g // GROUPS_PER_TILE
    eye = jnp.eye(GROUPS_PER_TILE, dtype=F32)

    def bmat(b):
        b4 = b.reshape(tiles, GROUPS_PER_TILE, n, k)
        return jnp.einsum('qgnk,gh->qgkhn', b4, eye).reshape(tiles, CH_TILE, STATES_PER_TILE)

    def cmat(c):
        c4 = c.reshape(tiles, GROUPS_PER_TILE, k, n)
        return jnp.einsum('qgkn,gh->qgnhk', c4, eye).reshape(tiles, STATES_PER_TILE, CH_TILE)

    chunks = STATES_PER_TILE // SCAN_LANES
    b_blk = jnp.stack([bmat(br).reshape(tiles, CH_TILE, chunks, SCAN_LANES),
                       bmat(bi).reshape(tiles, CH_TILE, chunks, SCAN_LANES)], axis=3)
    c_blk = jnp.stack([cmat(c_re).reshape(tiles, chunks, SCAN_LANES, CH_TILE),
                       -cmat(c_im).reshape(tiles, chunks, SCAN_LANES, CH_TILE)], axis=2)
    return (b_blk.reshape(tiles, CH_TILE, 2 * STATES_PER_TILE).astype(BF16),
            c_blk.reshape(tiles, 2 * STATES_PER_TILE, CH_TILE).astype(BF16))


def _s5_kernel(x_ref, g_ref, win_ref, b_ref, are_ref, aim_ref, c_ref, d_ref, wglu_ref,
               o_ref, st_ref, bu_ref, z_ref, *, tc, nb, d_model, batch_major_in):
    rows = tc * nb
    tiles = d_model // CH_TILE

    @pl.when(pl.program_id(0) == 0)
    def _():
        st_ref[...] = jnp.zeros_like(st_ref)

    x = x_ref[...]
    if batch_major_in:
        x = pltpu.einshape("btd->tbd", x)
    x = x.reshape(rows, d_model)
    h = _rms(x, g_ref[...]).astype(BF16)
    u = jnp.dot(h, win_ref[...], preferred_element_type=F32)
    u_bf = u.astype(BF16)

    for q in range(tiles):
        ch = slice(q * CH_TILE, (q + 1) * CH_TILE)
        y = d_ref[:, ch] * u[:, ch]
        for c in range(STATES_PER_TILE // SCAN_LANES):
            cols = slice(2 * c * SCAN_LANES, 2 * (c + 1) * SCAN_LANES)
            re = slice(2 * c * SCAN_LANES, (2 * c + 1) * SCAN_LANES)
            im = slice((2 * c + 1) * SCAN_LANES, (2 * c + 2) * SCAN_LANES)
            st = slice(c * SCAN_LANES, (c + 1) * SCAN_LANES)
            bu_ref[:, cols] = jnp.dot(u_bf[:, ch], b_ref[q, :, cols], preferred_element_type=F32)
            ar = jnp.broadcast_to(are_ref[q, :, st], (nb, SCAN_LANES))
            ai = jnp.broadcast_to(aim_ref[q, :, st], (nb, SCAN_LANES))
            sr, si = st_ref[q, :, re], st_ref[q, :, im]
            for t in range(tc):
                rws = slice(t * nb, (t + 1) * nb)
                nr = ar * sr - ai * si + bu_ref[rws, re]
                ni = ar * si + ai * sr + bu_ref[rws, im]
                bu_ref[rws, re] = nr
                bu_ref[rws, im] = ni
                sr, si = nr, ni
            st_ref[q, :, re] = sr
            st_ref[q, :, im] = si
            y = y + jnp.dot(bu_ref[:, cols].astype(BF16), c_ref[q, cols, :],
                            preferred_element_type=F32)
        z_ref[:, ch] = jax.nn.gelu(y).astype(BF16)

    glu = jnp.dot(z_ref[...], wglu_ref[...], preferred_element_type=F32)
    out = x + glu[:, :d_model] * jax.nn.sigmoid(glu[:, d_model:])
    o_ref[...] = out.reshape(tc, nb, d_model)


def _s5_layer(xin, gain, w_in, b_blk, a_re, a_im, c_blk, d_skip, w_glu, *, tc, batch_major_in):
    if batch_major_in:
        nb, seq, d = xin.shape
        x_spec = pl.BlockSpec((nb, tc, d), lambda i: (0, i, 0))
    else:
        seq, nb, d = xin.shape
        x_spec = pl.BlockSpec((tc, nb, d), lambda i: (i, 0, 0))
    tiles = d // CH_TILE
    rows = tc * nb
    kern = functools.partial(_s5_kernel, tc=tc, nb=nb, d_model=d, batch_major_in=batch_major_in)
    return pl.pallas_call(
        kern,
        out_shape=jax.ShapeDtypeStruct((seq, nb, d), F32),
        grid=(seq // tc,),
        in_specs=[
            x_spec,
            _const_spec((1, d)),
            _const_spec((d, d)),
            _const_spec((tiles, CH_TILE, 2 * STATES_PER_TILE)),
            _const_spec((tiles, 1, STATES_PER_TILE)),
            _const_spec((tiles, 1, STATES_PER_TILE)),
            _const_spec((tiles, 2 * STATES_PER_TILE, CH_TILE)),
            _const_spec((1, d)),
            _const_spec((d, 2 * d)),
        ],
        out_specs=pl.BlockSpec((tc, nb, d), lambda i: (i, 0, 0)),
        scratch_shapes=[
            pltpu.VMEM((tiles, nb, 2 * STATES_PER_TILE), F32),
            pltpu.VMEM((rows, 2 * STATES_PER_TILE), F32),
            pltpu.VMEM((rows, d), BF16),
        ],
        compiler_params=_params(("arbitrary",)),
        name="s5_layer",
    )(xin, gain.reshape(1, d), w_in.astype(BF16), b_blk,
      a_re.reshape(tiles, 1, STATES_PER_TILE), a_im.reshape(tiles, 1, STATES_PER_TILE),
      c_blk, d_skip.reshape(1, d), w_glu.astype(BF16))


FF_CHUNK = MXU_TILE


def _ffn_body(x, g_ref, wgu_ref, wd_ref, d_ff):
    h = _rms(x, g_ref[...]).astype(BF16)
    acc = jnp.zeros(x.shape, F32)
    for c in range(d_ff // FF_CHUNK):
        gate = jnp.dot(h, wgu_ref[:, c * FF_CHUNK:(c + 1) * FF_CHUNK], preferred_element_type=F32)
        up = jnp.dot(h, wgu_ref[:, d_ff + c * FF_CHUNK:d_ff + (c + 1) * FF_CHUNK],
                     preferred_element_type=F32)
        act = (jax.nn.silu(gate) * up).astype(BF16)
        acc = acc + jnp.dot(act, wd_ref[c * FF_CHUNK:(c + 1) * FF_CHUNK, :],
                            preferred_element_type=F32)
    return x + acc


def _ffn_kernel(x_ref, g_ref, wgu_ref, wd_ref, o_ref, *, d_ff):
    out = _ffn_body(x_ref[...], g_ref, wgu_ref, wd_ref, d_ff)
    if len(o_ref.shape) == 3:
        nb, steps, d = o_ref.shape
        out = pltpu.einshape("tbd->btd", out.reshape(steps, nb, d))
    o_ref[...] = out


def _proj_ffn_kernel(x_ref, a_ref, wo_ref, g_ref, wgu_ref, wd_ref, o_ref, *, d_ff):
    lane = lax.broadcasted_iota(jnp.int32, a_ref.shape[1:], 1)
    attn = jnp.concatenate([jnp.where(lane < V_DIM, a_ref[2 * p], a_ref[2 * p + 1])
                            for p in range(a_ref.shape[0] // 2)], axis=-1)
    x = x_ref[...] + jnp.dot(attn, wo_ref[...], preferred_element_type=F32)
    o_ref[...] = _ffn_body(x, g_ref, wgu_ref, wd_ref, d_ff)


def _ffn_layer(x2, gain, w_gate_up, w_down, *, tm, attn=None, w_o=None, batch_major_out=None):
    n, d = x2.shape
    d_ff = w_down.shape[0]
    row_spec = pl.BlockSpec((tm, d), lambda i: (i, 0))
    out_shape, out_spec = (n, d), row_spec
    if batch_major_out is not None:
        nb = batch_major_out
        out_shape = (nb, n // nb, d)
        out_spec = pl.BlockSpec((nb, tm // nb, d), lambda i: (0, i, 0))
    w_specs = [_const_spec((1, d)), _const_spec((d, 2 * d_ff)), _const_spec((d_ff, d))]
    w_args = (gain.reshape(1, d), w_gate_up.astype(BF16), w_down.astype(BF16))
    if attn is None:
        kern, in_specs, args = _ffn_kernel, [row_spec] + w_specs, (x2,) + w_args
    else:
        _, heads, seq, width = attn.shape
        per_b = seq // tm
        kern = _proj_ffn_kernel
        attn_spec = pl.BlockSpec((None, heads, tm, width), lambda i: (i // per_b, 0, i % per_b, 0))
        in_specs = [row_spec, attn_spec, _const_spec((heads // 2 * width, d))] + w_specs
        args = (x2, attn, w_o.astype(BF16)) + w_args
    return pl.pallas_call(
        functools.partial(kern, d_ff=d_ff),
        out_shape=jax.ShapeDtypeStruct(out_shape, F32),
        grid=(n // tm,),
        in_specs=in_specs,
        out_specs=out_spec,
        compiler_params=_params(("parallel",)),
        name="ffn_layer",
    )(*args)


def _rope_tables(pos_ref, freq_ref):
    ang = pos_ref[...].astype(F32) * freq_ref[...]
    return jnp.cos(ang), jnp.sin(ang)


def _rotate(tn, cos_t, sin_t, swap):
    partner = jnp.dot(tn.astype(BF16), swap, preferred_element_type=F32)
    return tn * cos_t + partner * sin_t


def _rope_swap_matrix(tiles=1):
    src = jnp.arange(HEAD_TILE)[:, None]
    dst = jnp.arange(HEAD_TILE)[None, :]
    x1_col = (dst >= QK_NOPE) & (dst < QK_NOPE + HALF_ROPE)
    x2_col = (dst >= QK_NOPE + HALF_ROPE) & (dst < QK_DIM)
    p = (jnp.where(x1_col & (src == dst + HALF_ROPE), -1.0, 0.0)
         + jnp.where(x2_col & (src == dst - HALF_ROPE), 1.0, 0.0))
    return jnp.kron(jnp.eye(tiles, dtype=F32), p).astype(BF16)


def _segment_mean_matrix(tiles=1):
    idx = jnp.arange(HEAD_TILE)
    seg = jnp.where(idx < QK_NOPE, 0, jnp.where(idx < QK_DIM, 1, 2))
    same = (seg[:, None] == seg[None, :]) & (seg[:, None] < 2)
    width = jnp.where(seg == 0, QK_NOPE, QK_ROPE).astype(F32)
    e = jnp.where(same, 1.0 / width[None, :], 0.0)
    return jnp.kron(jnp.eye(tiles, dtype=F32), e).astype(BF16)


def _head_tile_freq(inv_freq):
    z = jnp.zeros((QK_NOPE,), F32)
    pad = jnp.zeros((HEAD_TILE - QK_DIM,), F32)
    return jnp.concatenate([z, inv_freq, inv_freq, pad]).reshape(1, HEAD_TILE)


def _head_tile_gain(nope_gain, rope_gain):
    pad = jnp.zeros((HEAD_TILE - QK_DIM,), F32)
    return jnp.concatenate([nope_gain, rope_gain, pad]).reshape(1, HEAD_TILE)


def _kv_kernel(x_ref, pos_ref, freq_ref, g_ref, wa_ref, an_ref, wk_ref, wv_ref, e_ref, sw_ref,
               gt_ref, gc_ref, k_ref, v_ref, cos_ref, sin_ref):
    h = _rms(x_ref[...], g_ref[...]).astype(BF16)
    ckv = jnp.dot(h, wa_ref[...], preferred_element_type=F32)
    c_kv = _rms(ckv[:, :KV_LORA], an_ref[...]).astype(BF16)
    cos_t, sin_t = _rope_tables(pos_ref, freq_ref)
    cos_ref[...] = cos_t
    sin_ref[...] = sin_t
    gt = gt_ref[...]
    e = e_ref[...]

    kr = ckv[:, KV_LORA:]
    ms = jnp.dot((kr * kr).astype(BF16), e, preferred_element_type=F32)
    kr = _rotate(kr * lax.rsqrt(ms + EPS) * gt, cos_t, sin_t, sw_ref[...])
    lane = lax.broadcasted_iota(jnp.int32, kr.shape, 1)
    tm = kr.shape[0]

    kr_t = kr.T
    kn_t = lax.dot_general(wk_ref[...], c_kv, (((1,), (1,)), ((), ())),
                           preferred_element_type=F32)
    rowid = lax.broadcasted_iota(jnp.int32, (HEAD_TILE, tm), 0)
    gcol = jnp.broadcast_to(gc_ref[...], (HEAD_TILE, tm))
    for hd in range(N_HEADS):
        t = kn_t[hd * HEAD_TILE:(hd + 1) * HEAD_TILE, :]
        nope = t[:QK_NOPE]
        ms = jnp.sum(nope * nope, axis=0, keepdims=True) * (1.0 / QK_NOPE)
        tn = t * lax.rsqrt(ms + EPS) * gcol
        k_ref[hd] = jnp.where(rowid < QK_NOPE, tn, kr_t).astype(BF16)
    v = jnp.dot(c_kv, wv_ref[...], preferred_element_type=F32)
    for hd in range(N_HEADS):
        tile = v[:, hd * HEAD_TILE:(hd + 1) * HEAD_TILE]
        v_ref[hd] = jnp.where(lane == _ones_lane(hd % 2), 1.0, tile).astype(BF16)


def _ones_lane(parity):
    return V_DIM if parity == 0 else HEAD_TILE - V_DIM - 1


def _head_major_spec(heads, tm, seq):
    per_b = seq // tm
    return pl.BlockSpec((None, heads, tm, HEAD_TILE), lambda i: (i // per_b, 0, i % per_b, 0))


def _kv_proj(x2, pos2, inv_freq, gain, w_kv_a, kv_a_norm, w_kv_b, k_nope_norm, k_rope_norm, *,
             tm, bsz):
    n, d = x2.shape
    seq = n // bsz
    wa_rope = jnp.pad(w_kv_a[:, KV_LORA:], ((0, 0), (QK_NOPE, HEAD_TILE - QK_DIM)))
    wa = jnp.concatenate([w_kv_a[:, :KV_LORA], wa_rope], axis=1).astype(BF16)
    wb = w_kv_b.reshape(KV_LORA, N_HEADS, QK_NOPE + V_DIM)
    wk = jnp.pad(wb[:, :, :QK_NOPE], ((0, 0), (0, 0), (0, HEAD_TILE - QK_NOPE)))
    wk = wk.reshape(KV_LORA, N_HEADS * HEAD_TILE).T.astype(BF16)
    wv = wb[:, :, QK_NOPE:]
    odd_head = (jnp.arange(N_HEADS) % 2 == 1)[None, :, None]
    wv = jnp.where(odd_head, jnp.pad(wv, ((0, 0), (0, 0), (HEAD_TILE - V_DIM, 0))),
                   jnp.pad(wv, ((0, 0), (0, 0), (0, HEAD_TILE - V_DIM))))
    wv = wv.reshape(KV_LORA, N_HEADS * HEAD_TILE).astype(BF16)
    gain_tile = _head_tile_gain(k_nope_norm, k_rope_norm)
    row = lambda w: pl.BlockSpec((tm, w), lambda i: (i, 0))
    per_b = seq // tm
    kt_spec = pl.BlockSpec((None, N_HEADS, HEAD_TILE, tm), lambda i: (i // per_b, 0, 0, i % per_b))
    return pl.pallas_call(
        _kv_kernel,
        out_shape=[jax.ShapeDtypeStruct((bsz, N_HEADS, HEAD_TILE, seq), BF16),
                   jax.ShapeDtypeStruct((bsz, N_HEADS, seq, HEAD_TILE), BF16),
                   jax.ShapeDtypeStruct((n, HEAD_TILE), F32),
                   jax.ShapeDtypeStruct((n, HEAD_TILE), F32)],
        grid=(n // tm,),
        in_specs=[row(d), row(1), _const_spec((1, HEAD_TILE)), _const_spec((1, d)),
                  _const_spec(wa.shape), _const_spec((1, KV_LORA)), _const_spec(wk.shape),
                  _const_spec(wv.shape), _const_spec((HEAD_TILE, HEAD_TILE)),
                  _const_spec((HEAD_TILE, HEAD_TILE)),
                  _const_spec((1, HEAD_TILE)), _const_spec((HEAD_TILE, 1))],
        out_specs=[kt_spec, _head_major_spec(N_HEADS, tm, seq), row(HEAD_TILE), row(HEAD_TILE)],
        compiler_params=_params(("parallel",)),
        name="kv_proj",
    )(x2, pos2, _head_tile_freq(inv_freq), gain.reshape(1, d), wa, kv_a_norm.reshape(1, KV_LORA),
      wk, wv, _segment_mean_matrix(), _rope_swap_matrix(), gain_tile,
      gain_tile.reshape(HEAD_TILE, 1))


def _q_kernel(x_ref, cos_ref, sin_ref, g_ref, wa_ref, an_ref, wb_ref, e_ref, sw_ref, gt_ref, q_ref, *,
              scale):
    h = _rms(x_ref[...], g_ref[...]).astype(BF16)
    c_q = _rms(jnp.dot(h, wa_ref[...], preferred_element_type=F32), an_ref[...]).astype(BF16)
    cos2 = jnp.concatenate([cos_ref[...] * scale] * 2, axis=-1)
    sin2 = jnp.concatenate([sin_ref[...] * scale] * 2, axis=-1)
    gt2 = jnp.concatenate([gt_ref[...]] * 2, axis=-1)
    e = e_ref[...]
    swap = sw_ref[...]
    for pair in range(N_HEADS // 2):
        t2 = jnp.dot(c_q, wb_ref[:, 2 * pair * HEAD_TILE:(2 * pair + 2) * HEAD_TILE],
                     preferred_element_type=F32)
        ms2 = jnp.dot((t2 * t2).astype(BF16), e, preferred_element_type=F32)
        out = _rotate(t2 * lax.rsqrt(ms2 + EPS) * gt2, cos2, sin2, swap).astype(BF16)
        q_ref[2 * pair] = out[:, :HEAD_TILE]
        q_ref[2 * pair + 1] = out[:, HEAD_TILE:]


def _q_proj(x2, cos_t, sin_t, gain, w_q_a, q_a_norm, w_q_b, q_nope_norm, q_rope_norm, *, tm, bsz):
    n, d = x2.shape
    seq = n // bsz
    rank = w_q_a.shape[1]
    wb = w_q_b.reshape(rank, N_HEADS, QK_DIM)
    wb = jnp.pad(wb, ((0, 0), (0, 0), (0, HEAD_TILE - QK_DIM)))
    wb = wb.reshape(rank, N_HEADS * HEAD_TILE).astype(BF16)
    row = lambda w: pl.BlockSpec((tm, w), lambda i: (i, 0))
    return pl.pallas_call(
        functools.partial(_q_kernel, scale=math.log2(math.e) / math.sqrt(QK_DIM)),
        out_shape=jax.ShapeDtypeStruct((bsz, N_HEADS, seq, HEAD_TILE), BF16),
        grid=(n // tm,),
        in_specs=[row(d), row(HEAD_TILE), row(HEAD_TILE), _const_spec((1, d)),
                  _const_spec((d, rank)), _const_spec((1, rank)), _const_spec(wb.shape),
                  _const_spec((2 * HEAD_TILE, 2 * HEAD_TILE)),
                  _const_spec((2 * HEAD_TILE, 2 * HEAD_TILE)), _const_spec((1, HEAD_TILE))],
        out_specs=_head_major_spec(N_HEADS, tm, seq),
        compiler_params=_params(("parallel",)),
        name="q_proj",
    )(x2, cos_t, sin_t, gain.reshape(1, d), w_q_a.astype(BF16),
      q_a_norm.reshape(1, rank), wb, _segment_mean_matrix(2), _rope_swap_matrix(2),
      _head_tile_gain(q_nope_norm, q_rope_norm))


MAX_HEADS_IN_FLIGHT = 16
SCORE_SCRATCH_SEQS = 6


def _attn_kernel(q_ref, kt_ref, v_ref, o_ref, s_ref, m_ref, *, tq, nq):
    i = pl.program_id(1)
    nh = q_ref.shape[0]
    row = lax.broadcasted_iota(jnp.int32, (tq, tq), 0)
    col = lax.broadcasted_iota(jnp.int32, (tq, tq), 1)
    causal = col <= row
    lane_tiles = tq // LANES

    def stage_a(h, base, ms, nk):
        qh = q_ref[h]
        mx = None
        for j in range(nk):
            s = jnp.dot(qh, kt_ref[h, :, j * tq:(j + 1) * tq], preferred_element_type=F32)
            if j == nk - 1:
                s = jnp.where(causal, s, NEG_INF)
            s_ref[:, base + j * tq:base + (j + 1) * tq] = s
            for t in range(lane_tiles):
                f = s[:, t * LANES:(t + 1) * LANES]
                mx = f if mx is None else jnp.maximum(mx, f)
        m = jnp.max(mx, axis=-1, keepdims=True)
        m_ref[ms] = jnp.broadcast_to(m, (tq, LANES))

    def stage_b(h, base, ms, nk):
        m = m_ref[ms]
        m = jnp.concatenate([m] * lane_tiles, axis=-1)
        acc = None
        for j in range(nk):
            cols = slice(base + j * tq, base + (j + 1) * tq)
            p = jnp.exp2((s_ref[:, cols] - m).astype(BF16))
            pv = jnp.dot(p, v_ref[h, j * tq:(j + 1) * tq, :], preferred_element_type=F32)
            acc = pv if acc is None else acc + pv
        odd = (h % 2) == 1
        denom = jnp.where(odd, acc[:, _ones_lane(1):_ones_lane(1) + 1],
                          acc[:, _ones_lane(0):_ones_lane(0) + 1])
        o_ref[h] = (acc / denom).astype(BF16)

    for c in range(nq):
        @pl.when(i == c)
        def _(c=c):
            nk = c + 1
            hpi = min(m_ref.shape[0], s_ref.shape[1] // (nk * tq))
            hpi = 1 << (hpi.bit_length() - 1)

            def body(it, carry):
                for k in range(hpi):
                    stage_a(it * hpi + k, k * nk * tq, k, nk)
                for k in range(hpi):
                    stage_b(it * hpi + k, k * nk * tq, k, nk)
                return carry
            lax.fori_loop(0, nh // hpi, body, 0)


def _attention(q, kt, v, *, tq):
    bsz, nh, seq, _ = q.shape
    nq = seq // tq
    head_tile = lambda rows: pl.BlockSpec((None, nh, rows, HEAD_TILE), lambda b, i: (b, 0, i, 0))
    whole = lambda shape: pl.BlockSpec((None,) + shape, lambda b, i: (b, 0, 0, 0))
    return pl.pallas_call(
        functools.partial(_attn_kernel, tq=tq, nq=nq),
        out_shape=jax.ShapeDtypeStruct((bsz, nh, seq, HEAD_TILE), BF16),
        grid=(bsz, nq),
        in_specs=[head_tile(tq), whole((nh, HEAD_TILE, seq)), whole((nh, seq, HEAD_TILE))],
        out_specs=head_tile(tq),
        scratch_shapes=[pltpu.VMEM((tq, SCORE_SCRATCH_SEQS * seq), F32),
                        pltpu.VMEM((MAX_HEADS_IN_FLIGHT, tq, LANES), F32)],
        compiler_params=_params(("parallel", "arbitrary")),
        name="causal_attention",
    )(q, kt, v)


def _tiles(bsz, seq):
    tm = 512 if seq % 512 == 0 else seq
    tp = 1024 if seq % 1024 == 0 else tm
    tc = 16 if seq % 16 == 0 else seq
    tq = 256 if seq % 256 == 0 else seq
    assert tm % bsz == 0 and (tm // bsz) % 8 == 0, "an FFN row tile holds whole, sublane-aligned time steps"
    return tm, tp, tc, tq


def kernel(x, positions, mix_norm, ffn_norm, ffn_w_gate_up, ffn_w_down, ssm_w_in, ssm_lambda_re, ssm_lambda_im, ssm_log_step, ssm_b_re, ssm_b_im, ssm_c_re, ssm_c_im, ssm_d, ssm_w_glu, kv_in_norm, mla_w_kv_a, mla_kv_a_norm, mla_w_kv_b, mla_k_nope_norm, mla_k_rope_norm, mla_w_q_a, mla_q_a_norm, mla_w_q_b, mla_q_nope_norm, mla_q_rope_norm, mla_w_o):
    bsz, seq, d = x.shape
    n = bsz * seq
    depth = mix_norm.shape[0]
    n_ssm = ssm_w_in.shape[0]
    assert bsz % 8 == 0 and d % CH_TILE == 0
    tm, tp, tc, tq = _tiles(bsz, seq)

    xs = x
    for i in range(n_ssm):
        a_re, a_im, br, bi = _discretise(ssm_lambda_re[i], ssm_lambda_im[i], ssm_log_step[i],
                                         ssm_b_re[i], ssm_b_im[i])
        b_blk, c_blk = _block_diag_weights(br, bi, ssm_c_re[i], ssm_c_im[i])
        xs = _s5_layer(xs, mix_norm[i], ssm_w_in[i], b_blk, a_re, a_im, c_blk, ssm_d[i],
                       ssm_w_glu[i], tc=tc, batch_major_in=(i == 0))
        xs = _ffn_layer(xs.reshape(n, d), ffn_norm[i], ffn_w_gate_up[i], ffn_w_down[i], tm=tm,
                        batch_major_out=bsz if i == n_ssm - 1 else None)
        if i < n_ssm - 1:
            xs = xs.reshape(seq, bsz, d)
    x2 = xs.reshape(n, d)

    pos2 = positions.reshape(n, 1)
    inv_freq = ROPE_THETA ** (-jnp.arange(0, QK_ROPE, 2, dtype=F32) / QK_ROPE)
    k, v, cos_t, sin_t = _kv_proj(x2, pos2, inv_freq, kv_in_norm, mla_w_kv_a, mla_kv_a_norm,
                                  mla_w_kv_b, mla_k_nope_norm, mla_k_rope_norm, tm=tp, bsz=bsz)
    for layer in range(n_ssm, depth):
        j = layer - n_ssm
        q = _q_proj(x2, cos_t, sin_t, mix_norm[layer], mla_w_q_a[j], mla_q_a_norm[j],
                    mla_w_q_b[j], mla_q_nope_norm[j], mla_q_rope_norm[j], tm=tp, bsz=bsz)
        o = _attention(q, k, v, tq=tq)
        x2 = _ffn_layer(x2, ffn_norm[layer], ffn_w_gate_up[layer], ffn_w_down[layer], tm=tm,
                        attn=o, w_o=mla_w_o[j])
    return x2.reshape(bsz, seq, d)
```
